```python
import math
import jax
import jax.numpy as jnp
from jax import lax
import numpy as np

D_MODEL = 2048
BATCH = 4
SEQ = 4096
DEPTH = 1

GRID_W = 64
CTX_LEN = 256
N_MOD = 9
MACARON_W = 0.5
D_FF = 5632
RMS_EPS = 1e-6
GN_EPS = 64e-5
L2_EPS = 1e-12

RW_HEAD = 64
RW_HEADS = D_MODEL // RW_HEAD
RW_DIM = RW_HEADS * RW_HEAD
W_LORA = 96
A_LORA = 96
G_LORA = 256
RW_COLS = 3 * RW_DIM + W_LORA + A_LORA + G_LORA

M_DIM = 2 * D_MODEL
M_HEADDIM = 64
M_HEADS = M_DIM // M_HEADDIM
M_GROUPS = 8
M_STATE = 128
M_CONV = 3
M_CHUNK = 64
M_BC = M_GROUPS * M_STATE
M_XBC = M_DIM + 2 * M_BC
M_COLS = M_DIM + M_XBC + 2 * M_HEADS

GATE_COLS = 2 * D_MODEL
IN_COLS = RW_COLS + M_COLS + GATE_COLS

kernel_name = "hybrid_rwkv7_mamba2_flow_block"


def _split(t, sizes):
    return jnp.split(t, np.cumsum(sizes)[:-1].tolist(), axis=-1)


def _rmsnorm(t, g):
    tf = t.astype(jnp.float32)
    y = tf * lax.rsqrt(jnp.mean(tf * tf, axis=-1, keepdims=True) + RMS_EPS)
    return (y * g.astype(jnp.float32)).astype(t.dtype)


def _modulated_norm(h, mod, i, g_pre):
    return _rmsnorm(h, g_pre) * (1 + mod[:, :, 3 * i + 1]) + mod[:, :, 3 * i]


def _residual_add(h, y, mod, i, g_post, weight):
    return h + weight * mod[:, :, 3 * i + 2] * _rmsnorm(y, g_post)


def _swiglu(t, w_gate, w_up, w_down):
    return (jax.nn.silu(t @ w_gate) * (t @ w_up)) @ w_down


def _ffn_sublayer(h, mod, i, g_pre, g_post, w_gate, w_up, w_down):
    xn = _modulated_norm(h, mod, i, g_pre)
    return _residual_add(h, _swiglu(xn, w_gate, w_up, w_down), mod, i, g_post, MACARON_W)


def _q_shift(t):
    b, l, ch = t.shape
    rows = l // GRID_W
    q = ch // 4
    g = jnp.pad(t.reshape(b, rows, GRID_W, ch), ((0, 0), (1, 1), (1, 1), (0, 0)))
    left = g[:, 1:-1, :-2, :q]
    right = g[:, 1:-1, 2:, q:2 * q]
    up = g[:, :-2, 1:-1, 2 * q:3 * q]
    down = g[:, 2:, 1:-1, 3 * q:]
    return jnp.concatenate([left, right, up, down], axis=-1).reshape(b, l, ch)


def _bi_shift(t):
    hch = t.shape[-1] // 2
    prev = jnp.pad(t[:, :-1, :hch], ((0, 0), (1, 0), (0, 0)))
    nxt = jnp.pad(t[:, 1:, hch:], ((0, 0), (0, 1), (0, 0)))
    return jnp.concatenate([prev, nxt], axis=-1)


def _dwconv_centred(t, w, bias):
    pad = w.shape[0] // 2
    y = lax.conv_general_dilated(t, w[:, None, :].astype(t.dtype), window_strides=(1,),
                                 padding=[(pad, pad)], dimension_numbers=('NWC', 'WIO', 'NWC'),
                                 feature_group_count=t.shape[-1])
    return y + bias


def _heads(t):
    return t.reshape(t.shape[0], t.shape[1], RW_HEADS, RW_HEAD)


def _rw_decay(z):
    z = z.astype(jnp.float32)
    return jnp.exp(-jnp.exp(-jax.nn.softplus(-z) - 0.5))


def _rwkv7_features(u, u_shift, rw_mu, rw_w0, rw_w_up, rw_a0, rw_a_up, rw_k_k, rw_k_a):
    u = u + (u_shift - u) * rw_mu
    r, k, v, wl, al, gl = _split(u, [RW_DIM, RW_DIM, RW_DIM, W_LORA, A_LORA, G_LORA])
    a = jax.nn.sigmoid(rw_a0 + al @ rw_a_up)
    kk = _heads(k * rw_k_k).astype(jnp.float32)
    kk = kk / jnp.maximum(jnp.sqrt(jnp.sum(kk * kk, axis=-1, keepdims=True)), L2_EPS)
    k = k * (1 + (a - 1) * rw_k_a)
    tw = jnp.tanh(wl)
    w_f = _rw_decay(rw_w0[0] + tw @ rw_w_up[0])
    w_b = _rw_decay(rw_w0[1] + tw @ rw_w_up[1])
    return (_heads(r), _heads(k), _heads(v), kk, _heads(a), _heads(w_f), _heads(w_b), gl)


def _rwkv7_scan(S0, r, k, v, kk, a, w, reverse, emit):
    def step(S, inp):
        r_t, k_t, v_t, kk_t, a_t, w_t = inp
        sa = jnp.einsum('bhvk,bhk->bhv', S, kk_t)
        S = (S * w_t[:, :, None, :] - sa[..., None] * (kk_t * a_t)[:, :, None, :]
             + v_t[..., :, None] * k_t[:, :, None, :])
        y = jnp.einsum('bhvk,bhk->bhv', S, r_t) if emit else None
        return S, y
    xs = (jnp.moveaxis(r, 1, 0), jnp.moveaxis(k, 1, 0), jnp.moveaxis(v, 1, 0),
          jnp.moveaxis(kk, 1, 0), jnp.moveaxis(a, 1, 0), jnp.moveaxis(w, 1, 0))
    S, ys = lax.scan(step, S0, xs, reverse=reverse)
    return (jnp.moveaxis(ys, 0, 1) if emit else None), S


def _rwkv7_output(y, r, k, v, gl, rw_g_up, rw_r_k, rw_ln_g, rw_ln_b, w_br):
    b, l = y.shape[:2]
    mu = jnp.mean(y, axis=-1, keepdims=True)
    var = jnp.mean(jnp.square(y - mu), axis=-1, keepdims=True)
    yn = ((y - mu) * lax.rsqrt(var + GN_EPS)).reshape(b, l, RW_DIM)
    bonus = (jnp.sum(r * k * rw_r_k, axis=-1, keepdims=True) * v).reshape(b, l, RW_DIM)
    g = jax.nn.sigmoid(gl) @ rw_g_up
    o = (yn * rw_ln_g + rw_ln_b + bonus) * g
    return o.astype(r.dtype) @ w_br


def _mamba2_features(um, m_conv_w, m_conv_b, m_dt_bias):
    b, l, _ = um.shape
    z, xbc, dtr_f, dtr_b = _split(um, [M_DIM, M_XBC, M_HEADS, M_HEADS])
    xbc = jax.nn.silu(_dwconv_centred(xbc, m_conv_w, m_conv_b))
    xs, Bm, Cm = _split(xbc, [M_DIM, M_BC, M_BC])
    xs = xs.reshape(b, l, M_HEADS, M_HEADDIM)
    Bm = Bm.reshape(b, l, M_GROUPS, M_STATE)
    Cm = Cm.reshape(b, l, M_GROUPS, M_STATE)
    dt_f = jax.nn.softplus((dtr_f + m_dt_bias[0]).astype(jnp.float32))
    dt_b = jax.nn.softplus((dtr_b + m_dt_bias[1]).astype(jnp.float32))
    return z, xs, Bm, Cm, dt_f, dt_b


def _ssd(xs, dt, A, Bm, Cm, S0, emit):
    b, l = xs.shape[:2]
    nc = l // M_CHUNK
    e = M_HEADS // M_GROUPS
    X = (xs * dt[..., None]).reshape(b, nc, M_CHUNK, M_GROUPS, e, M_HEADDIM)
    dA = (dt * A).reshape(b, nc, M_CHUNK, M_GROUPS, e)
    Bc = Bm.reshape(b, nc, M_CHUNK, M_GROUPS, M_STATE)
    Cc = Cm.reshape(b, nc, M_CHUNK, M_GROUPS, M_STATE)
    cum = jnp.cumsum(dA, axis=2)
    to_end = jnp.exp(cum[:, :, -1:] - cum)
    chunk_states = jnp.einsum('bcsgn,bcsge,bcsgep->bcgepn', Bc, to_end, X)
    chunk_decay = jnp.exp(cum[:, :, -1])

    def carry_step(S, inp):
        st, dec = inp
        return S * dec[..., None, None] + st, S
    S_fin, S_in = lax.scan(carry_step, S0, (jnp.moveaxis(chunk_states, 1, 0),
                                            jnp.moveaxis(chunk_decay, 1, 0)))
    if not emit:
        return None, S_fin
    S_in = jnp.moveaxis(S_in, 0, 1)
    lower = jnp.tril(jnp.ones((M_CHUNK, M_CHUNK), dtype=bool))[:, :, None, None]
    seg = cum[:, :, :, None] - cum[:, :, None, :]
    Lmat = jnp.exp(jnp.where(lower, seg, -jnp.inf))
    CB = jnp.einsum('bctgn,bcsgn->bcgts', Cc, Bc)
    y_diag = jnp.einsum('bcgts,bctsge,bcsgep->bctgep', CB, Lmat, X)
    y_off = jnp.einsum('bctgn,bcgepn,bctge->bctgep', Cc, S_in, jnp.exp(cum))
    return (y_diag + y_off).reshape(b, l, M_HEADS, M_HEADDIM), S_fin


def _ssd_dir(xs, dt, A, Bm, Cm, S0, reverse, emit):
    if reverse:
        xs, dt, Bm, Cm = jnp.flip(xs, 1), jnp.flip(dt, 1), jnp.flip(Bm, 1), jnp.flip(Cm, 1)
    y, S = _ssd(xs, dt, A, Bm, Cm, S0, emit)
    if reverse and emit:
        y = jnp.flip(y, 1)
    return y, S


def _mamba2_output(y, xs, z, m_D, m_norm, w_bm):
    b, l = xs.shape[:2]
    y = (y + m_D[:, None] * xs).reshape(b, l, M_DIM)
    y = _rmsnorm(y * jax.nn.silu(z.astype(jnp.float32)), m_norm)
    return y.astype(z.dtype) @ w_bm


def _merge(br, bm, ug, w_out):
    g_r, g_m = _split(ug, [D_MODEL, D_MODEL])
    return (jax.nn.sigmoid(g_r) * br + jax.nn.sigmoid(g_m) * bm) @ w_out


def _token_mix(xn, xnc, w_in, rw_mu, rw_w0, rw_w_up, rw_a0, rw_a_up, rw_g_up, rw_k_k, rw_k_a,
               rw_r_k, rw_ln_g, rw_ln_b, m_conv_w, m_conv_b, m_dt_bias, m_A_log, m_D, m_norm,
               w_br, w_bm, w_out, emit_ctx):
    b = xn.shape[0]
    ur, um, ug = _split(xn @ w_in, [RW_COLS, M_COLS, GATE_COLS])
    urc, umc, ugc = _split(xnc @ w_in, [RW_COLS, M_COLS, GATE_COLS])

    r, k, v, kk, a, wf, wb, gl = _rwkv7_features(ur, _q_shift(ur), rw_mu, rw_w0, rw_w_up,
                                                 rw_a0, rw_a_up, rw_k_k, rw_k_a)
    rc, kc, vc, kkc, ac, wfc, wbc, glc = _rwkv7_features(urc, _bi_shift(urc), rw_mu, rw_w0,
                                                         rw_w_up, rw_a0, rw_a_up, rw_k_k, rw_k_a)
    S0r = jnp.zeros((b, RW_HEADS, RW_HEAD, RW_HEAD), jnp.float32)
    yrc_f, Src_f = _rwkv7_scan(S0r, rc, kc, vc, kkc, ac, wfc, False, emit_ctx)
    yrc_b, Src_b = _rwkv7_scan(S0r, rc, kc, vc, kkc, ac, wbc, True, emit_ctx)
    yr_f, _ = _rwkv7_scan(Src_f, r, k, v, kk, a, wf, False, True)
    yr_b, _ = _rwkv7_scan(Src_b, r, k, v, kk, a, wb, True, True)
    br = _rwkv7_output(yr_f + yr_b, r, k, v, gl, rw_g_up, rw_r_k, rw_ln_g, rw_ln_b, w_br)

    z, xs, Bm, Cm, dtf, dtb = _mamba2_features(um, m_conv_w, m_conv_b, m_dt_bias)
    zc, xsc, Bmc, Cmc, dtfc, dtbc = _mamba2_features(umc, m_conv_w, m_conv_b, m_dt_bias)
    A_f = -jnp.exp(m_A_log[0].astype(jnp.float32))
    A_b = -jnp.exp(m_A_log[1].astype(jnp.float32))
    S0m = jnp.zeros((b, M_GROUPS, M_HEADS // M_GROUPS, M_HEADDIM, M_STATE), jnp.float32)
    ymc_f, Smc_f = _ssd_dir(xsc, dtfc, A_f, Bmc, Cmc, S0m, False, emit_ctx)
    ymc_b, Smc_b = _ssd_dir(xsc, dtbc, A_b, Bmc, Cmc, S0m, True, emit_ctx)
    ym_f, _ = _ssd_dir(xs, dtf, A_f, Bm, Cm, Smc_f, False, True)
    ym_b, _ = _ssd_dir(xs, dtb, A_b, Bm, Cm, Smc_b, True, True)
    bm = _mamba2_output(ym_f + ym_b, xs, z, m_D, m_norm, w_bm)

    y = _merge(br, bm, ug, w_out)
    if not emit_ctx:
        return y, None
    brc = _rwkv7_output(yrc_f + yrc_b, rc, kc, vc, glc, rw_g_up, rw_r_k, rw_ln_g, rw_ln_b, w_br)
    bmc = _mamba2_output(ymc_f + ymc_b, xsc, zc, m_D, m_norm, w_bm)
    return y, _merge(brc, bmc, ugc, w_out)


def setup_inputs(seed: int = 0) -> dict:
    key = jax.random.key(seed)
    ks = list(jax.random.split(key, 40))

    def nrm(shape, scale):
        return scale * jax.random.normal(ks.pop(), shape, jnp.float32)

    def unif(shape, lo, hi):
        return jax.random.uniform(ks.pop(), shape, jnp.float32, lo, hi)

    dt0 = jnp.exp(unif((DEPTH, 2, M_HEADS), math.log(1e-3), math.log(1e-1)))
    return {
        "x": nrm((BATCH, SEQ, D_MODEL), 1.0),
        "c": nrm((BATCH, D_MODEL), 1.0),
        "ctx": nrm((BATCH, CTX_LEN, D_MODEL), 1.0),
        "c_ctx": nrm((D_MODEL,), 1.0),
        "ada_w": nrm((DEPTH, D_MODEL, N_MOD * D_MODEL), D_MODEL ** -0.5),
        "ada_b": nrm((DEPTH, N_MOD * D_MODEL), 0.02),
        "norm_pre": 1.0 + nrm((DEPTH, 3, D_MODEL), 0.05),
        "norm_post": 1.0 + nrm((DEPTH, 3, D_MODEL), 0.05),
        "ffn1_gate": nrm((DEPTH, D_MODEL, D_FF), D_MODEL ** -0.5),
        "ffn1_up": nrm((DEPTH, D_MODEL, D_FF), D_MODEL ** -0.5),
        "ffn1_down": nrm((DEPTH, D_FF, D_MODEL), D_FF ** -0.5),
        "ffn2_gate": nrm((DEPTH, D_MODEL, D_FF), D_MODEL ** -0.5),
        "ffn2_up": nrm((DEPTH, D_MODEL, D_FF), D_MODEL ** -0.5),
        "ffn2_down": nrm((DEPTH, D_FF, D_MODEL), D_FF ** -0.5),
        "w_in": nrm((DEPTH, D_MODEL, IN_COLS), D_MODEL ** -0.5),
        "rw_mu": unif((DEPTH, RW_COLS), 0.0, 1.0),
        "rw_w0": nrm((DEPTH, 2, RW_DIM), 0.5),
        "rw_w_up": nrm((DEPTH, 2, W_LORA, RW_DIM), W_LORA ** -0.5),
        "rw_a0": nrm((DEPTH, RW_DIM), 0.5),
        "rw_a_up": nrm((DEPTH, A_LORA, RW_DIM), A_LORA ** -0.5),
        "rw_g_up": nrm((DEPTH, G_LORA, RW_DIM), G_LORA ** -0.5),
        "rw_k_k": 0.85 + nrm((DEPTH, RW_DIM), 0.05),
        "rw_k_a": 1.0 + nrm((DEPTH, RW_DIM), 0.05),
        "rw_r_k": nrm((DEPTH, RW_HEADS, RW_HEAD), 0.1),
        "rw_ln_g": 1.0 + nrm((DEPTH, RW_DIM), 0.05),
        "rw_ln_b": nrm((DEPTH, RW_DIM), 0.02),
        "m_conv_w": nrm((DEPTH, M_CONV, M_XBC), M_CONV ** -0.5),
        "m_conv_b": nrm((DEPTH, M_XBC), 0.02),
        "m_dt_bias": dt0 + jnp.log(-jnp.expm1(-dt0)),
        "m_A_log": jnp.log(unif((DEPTH, 2, M_HEADS), 1.0, 16.0)),
        "m_D": 1.0 + nrm((DEPTH, M_HEADS), 0.05),
        "m_norm": 1.0 + nrm((DEPTH, M_DIM), 0.05),
        "w_br": nrm((DEPTH, RW_DIM, D_MODEL), RW_DIM ** -0.5),
        "w_bm": nrm((DEPTH, M_DIM, D_MODEL), M_DIM ** -0.5),
        "w_out": nrm((DEPTH, D_MODEL, D_MODEL), D_MODEL ** -0.5),
    }


def reference(x, c, ctx, c_ctx, ada_w, ada_b, norm_pre, norm_post,
              ffn1_gate, ffn1_up, ffn1_down, ffn2_gate, ffn2_up, ffn2_down,
              w_in, rw_mu, rw_w0, rw_w_up, rw_a0, rw_a_up, rw_g_up, rw_k_k, rw_k_a,
              rw_r_k, rw_ln_g, rw_ln_b, m_conv_w, m_conv_b, m_dt_bias, m_A_log, m_D,
              m_norm, w_br, w_bm, w_out):
    h, hc = x, ctx
    for i in range(DEPTH):
        emit_ctx = i < DEPTH - 1
        mod = (jax.nn.silu(c) @ ada_w[i] + ada_b[i]).reshape(c.shape[0], 1, N_MOD, D_MODEL)
        mod_c = (jax.nn.silu(c_ctx) @ ada_w[i] + ada_b[i]).reshape(1, 1, N_MOD, D_MODEL)
        h = _ffn_sublayer(h, mod, 0, norm_pre[i, 0], norm_post[i, 0], ffn1_gate[i], ffn1_up[i], ffn1_down[i])
        hc = _ffn_sublayer(hc, mod_c, 0, norm_pre[i, 0], norm_post[i, 0], ffn1_gate[i], ffn1_up[i], ffn1_down[i])
        y, yc = _token_mix(_modulated_norm(h, mod, 1, norm_pre[i, 1]),
                           _modulated_norm(hc, mod_c, 1, norm_pre[i, 1]),
                           w_in[i], rw_mu[i], rw_w0[i], rw_w_up[i], rw_a0[i], rw_a_up[i],
                           rw_g_up[i], rw_k_k[i], rw_k_a[i], rw_r_k[i], rw_ln_g[i], rw_ln_b[i],
                           m_conv_w[i], m_conv_b[i], m_dt_bias[i], m_A_log[i], m_D[i], m_norm[i],
                           w_br[i], w_bm[i], w_out[i], emit_ctx)
        h = _residual_add(h, y, mod, 1, norm_post[i, 1], 1.0)
        h = _ffn_sublayer(h, mod, 2, norm_pre[i, 2], norm_post[i, 2], ffn2_gate[i], ffn2_up[i], ffn2_down[i])
        if emit_ctx:
            hc = _residual_add(hc, yc, mod_c, 1, norm_post[i, 1], 1.0)
            hc = _ffn_sublayer(hc, mod_c, 2, norm_pre[i, 2], norm_post[i, 2], ffn2_gate[i], ffn2_up[i], ffn2_down[i])
    return h
```

```python
import functools
import math

import jax
import jax.numpy as jnp
import numpy as np
from jax import lax
from jax.experimental import pallas as pl
from jax.experimental.pallas import tpu as pltpu

F32 = jnp.float32
BF16 = jnp.bfloat16
HIGHEST = lax.Precision.HIGHEST

D_MODEL = 2048
GRID_W = 64
N_MOD = 9
MACARON_W = 0.5
D_FF = 5632
RMS_EPS = 1e-6
GN_EPS = 64e-5
L2_EPS = 1e-12

RW_HEAD = 64
RW_HEADS = D_MODEL // RW_HEAD
RW_DIM = RW_HEADS * RW_HEAD
W_LORA = 96
A_LORA = 96
G_LORA = 256
RW_COLS = 3 * RW_DIM + W_LORA + A_LORA + G_LORA

M_DIM = 2 * D_MODEL
M_HEADDIM = 64
M_HEADS = M_DIM // M_HEADDIM
M_GROUPS = 8
M_HPG = M_HEADS // M_GROUPS
M_STATE = 128
M_BC = M_GROUPS * M_STATE
M_XBC = M_DIM + 2 * M_BC
M_COLS = M_DIM + M_XBC + 2 * M_HEADS
GATE_COLS = 2 * D_MODEL

CHUNK = 64
LANES = 128
VMEM_LIMIT = 48 * 1024 * 1024


def _cparams(sem):
    return pltpu.CompilerParams(dimension_semantics=sem, vmem_limit_bytes=VMEM_LIMIT)


def _dot(a, b):
    return jnp.dot(a.astype(BF16), b.astype(BF16), preferred_element_type=F32)


def _dot_nt(a, b):
    return lax.dot_general(a.astype(BF16), b.astype(BF16), (((1,), (1,)), ((), ())),
                           preferred_element_type=F32)


def _dot_tn(a, b):
    return lax.dot_general(a.astype(BF16), b.astype(BF16), (((0,), (0,)), ((), ())),
                           preferred_element_type=F32)


def _dot_f32(a, b):
    return jnp.dot(a, b, precision=HIGHEST, preferred_element_type=F32)


def _mm_kernel(a_ref, b_ref, o_ref):
    o_ref[...] = _dot(a_ref[...], b_ref[...]).astype(o_ref.dtype)


def _mm(a, b, tm, tn, out_dtype=F32):
    m, k = a.shape
    _, n = b.shape
    assert m % tm == 0 and n % tn == 0, (m, n, tm, tn)
    return pl.pallas_call(
        _mm_kernel,
        grid=(m // tm, n // tn),
        in_specs=[pl.BlockSpec((tm, k), lambda i, j: (i, 0)),
                  pl.BlockSpec((k, tn), lambda i, j: (0, j))],
        out_specs=pl.BlockSpec((tm, tn), lambda i, j: (i, j)),
        out_shape=jax.ShapeDtypeStruct((m, n), out_dtype),
        compiler_params=_cparams(("parallel", "arbitrary")),
        name="mm",
    )(a, b)


def _rms(t):
    return t * lax.rsqrt(jnp.mean(t * t, axis=-1, keepdims=True) + RMS_EPS)


def _ffn_kernel(h_ref, mod_ref, gpre_ref, gpost_ref, wg_ref, wu_ref, wd_ref, o_ref,
                xn_ref, acc_ref, *, sub, weight):
    j = pl.program_id(1)

    @pl.when(j == 0)
    def _():
        y = _rms(h_ref[...]) * gpre_ref[...]
        xn = y * (1.0 + mod_ref[0, 3 * sub + 1:3 * sub + 2, :]) + mod_ref[0, 3 * sub:3 * sub + 1, :]
        xn_ref[...] = xn.astype(BF16)
        acc_ref[...] = jnp.zeros_like(acc_ref)

    xn = xn_ref[...]
    g = jnp.dot(xn, wg_ref[...], preferred_element_type=F32)
    u = jnp.dot(xn, wu_ref[...], preferred_element_type=F32)
    a = (g * jax.nn.sigmoid(g)) * u
    acc_ref[...] += jnp.dot(a.astype(BF16), wd_ref[...], preferred_element_type=F32)

    @pl.when(j == pl.num_programs(1) - 1)
    def _():
        yn = _rms(acc_ref[...]) * gpost_ref[...]
        o_ref[...] = h_ref[...] + (weight * mod_ref[0, 3 * sub + 2:3 * sub + 3, :]) * yn


def _ffn(h, mod, mod_row, sub, g_pre, g_post, wg, wu, wd, tm, tf):
    m, d = h.shape
    f = wg.shape[1]
    assert m % tm == 0 and f % tf == 0
    return pl.pallas_call(
        functools.partial(_ffn_kernel, sub=sub, weight=MACARON_W),
        grid=(m // tm, f // tf),
        in_specs=[pl.BlockSpec((tm, d), lambda i, j: (i, 0)),
                  pl.BlockSpec((1, N_MOD, d), lambda i, j: (mod_row(i), 0, 0)),
                  pl.BlockSpec((1, d), lambda i, j: (0, 0)),
                  pl.BlockSpec((1, d), lambda i, j: (0, 0)),
                  pl.BlockSpec((d, tf), lambda i, j: (0, j)),
                  pl.BlockSpec((d, tf), lambda i, j: (0, j)),
                  pl.BlockSpec((tf, d), lambda i, j: (j, 0))],
        out_specs=pl.BlockSpec((tm, d), lambda i, j: (i, 0)),
        out_shape=jax.ShapeDtypeStruct((m, d), F32),
        scratch_shapes=[pltpu.VMEM((tm, d), BF16), pltpu.VMEM((tm, d), F32)],
        compiler_params=_cparams(("parallel", "arbitrary")),
        name="ffn",
    )(h, mod, g_pre.reshape(1, d), g_post.reshape(1, d), wg, wu, wd)


def _norm_mm_kernel(h_ref, mod_ref, gpre_ref, w_ref, o_ref, xn_ref, *, sub):
    @pl.when(pl.program_id(1) == 0)
    def _():
        y = _rms(h_ref[...]) * gpre_ref[...]
        xn = y * (1.0 + mod_ref[0, 3 * sub + 1:3 * sub + 2, :]) + mod_ref[0, 3 * sub:3 * sub + 1, :]
        xn_ref[...] = xn.astype(BF16)

    o_ref[...] = jnp.dot(xn_ref[...], w_ref[...], preferred_element_type=F32)


def _norm_mm(h, mod, mod_row, sub, g_pre, w, tm, tn):
    m, d = h.shape
    n = w.shape[1]
    assert m % tm == 0 and n % tn == 0
    return pl.pallas_call(
        functools.partial(_norm_mm_kernel, sub=sub),
        grid=(m // tm, n // tn),
        in_specs=[pl.BlockSpec((tm, d), lambda i, j: (i, 0)),
                  pl.BlockSpec((1, N_MOD, d), lambda i, j: (mod_row(i), 0, 0)),
                  pl.BlockSpec((1, d), lambda i, j: (0, 0)),
                  pl.BlockSpec((d, tn), lambda i, j: (0, j))],
        out_specs=pl.BlockSpec((tm, tn), lambda i, j: (i, j)),
        out_shape=jax.ShapeDtypeStruct((m, n), F32),
        scratch_shapes=[pltpu.VMEM((tm, d), BF16)],
        compiler_params=_cparams(("parallel", "arbitrary")),
        name="norm_mm",
    )(h, mod, g_pre.reshape(1, d), w)


def _resid_kernel(h_ref, y_ref, mod_ref, gpost_ref, o_ref, *, sub, weight):
    yn = _rms(y_ref[...]) * gpost_ref[...]
    o_ref[...] = h_ref[...] + (weight * mod_ref[0, 3 * sub + 2:3 * sub + 3, :]) * yn


def _resid(h, y, mod, mod_row, sub, g_post, weight, tm):
    m, d = h.shape
    return pl.pallas_call(
        functools.partial(_resid_kernel, sub=sub, weight=weight),
        grid=(m // tm,),
        in_specs=[pl.BlockSpec((tm, d), lambda i: (i, 0)),
                  pl.BlockSpec((tm, d), lambda i: (i, 0)),
                  pl.BlockSpec((1, N_MOD, d), lambda i: (mod_row(i), 0, 0)),
                  pl.BlockSpec((1, d), lambda i: (0, 0))],
        out_specs=pl.BlockSpec((tm, d), lambda i: (i, 0)),
        out_shape=jax.ShapeDtypeStruct((m, d), F32),
        compiler_params=_cparams(("parallel",)),
        name="resid",
    )(h, y, mod, g_post.reshape(1, d))


def _stack2(x, m0):
    return jnp.concatenate([jnp.where(m0, x, 0.0), jnp.where(m0, 0.0, x)], axis=0)


def _rwkv_kernel(r_ref, k_ref, v_ref, kk_ref, b_ref, lw_ref, s0_ref, y_ref, sfin_ref, s_scr,
                 *, nchunks):
    i = pl.program_id(2)

    @pl.when(i == 0)
    def _():
        s_scr[...] = s0_ref[...]

    m0 = lax.broadcasted_iota(jnp.int32, (CHUNK, LANES), 1) < RW_HEAD
    row = lax.broadcasted_iota(jnp.int32, (2 * CHUNK, LANES), 0)
    col = lax.broadcasted_iota(jnp.int32, (2 * CHUNK, LANES), 1)
    same = (row // CHUNK) == (col // RW_HEAD)
    strict = same & ((col % RW_HEAD) < (row % CHUNK))
    incl = same & ((col % RW_HEAD) <= (row % CHUNK))
    eye = (row == col).astype(F32)
    tr = lax.broadcasted_iota(jnp.int32, (CHUNK, CHUNK), 0)
    tc = lax.broadcasted_iota(jnp.int32, (CHUNK, CHUNK), 1)
    tril = (tr >= tc).astype(F32)

    def chunk(c, s):
        sl = pl.ds(pl.multiple_of(c * CHUNK, CHUNK), CHUNK)
        lw = lw_ref[sl, :]
        r = r_ref[sl, :]
        k = k_ref[sl, :]
        v = v_ref[sl, :]
        kk = kk_ref[sl, :]
        b = b_ref[sl, :]
        cum = _dot_f32(tril, lw)
        tot = cum[CHUNK - 1:CHUNK, :]
        eneg = jnp.exp(-cum)
        a2 = _stack2(-kk * jnp.exp(cum - lw), m0)
        r2 = _stack2(r * jnp.exp(cum), m0)
        bt = b * eneg
        kt = k * eneg
        bb = jnp.concatenate([bt, bt], axis=0)
        kb = jnp.concatenate([kt, kt], axis=0)
        n = jnp.where(strict, _dot_nt(a2, bb), 0.0)
        ak = jnp.where(strict, _dot_nt(a2, kb), 0.0)
        rb = jnp.where(incl, _dot_nt(r2, bb), 0.0)
        rk = jnp.where(incl, _dot_nt(r2, kb), 0.0)
        v2 = _stack2(v, m0)
        p = eye + n
        npow = n
        for _ in range(5):
            npow = _dot(npow, npow)
            p = p + _dot(p, npow)
        akv = _dot(ak, v2)
        w2 = _dot(p, a2)
        up2 = _dot(p, akv)
        u2 = _dot_nt(w2, s) + up2
        y2 = _dot_nt(r2, s) + _dot(rb, u2) + _dot(rk, v2)
        y_ref[sl, :] = y2[:CHUNK] + y2[CHUNK:]
        eend = jnp.exp(tot - cum)
        bg2 = _stack2(b * eend, m0)
        kg2 = _stack2(k * eend, m0)
        return s * jnp.exp(tot) + _dot_tn(u2, bg2) + _dot_tn(v2, kg2)

    s = lax.fori_loop(0, nchunks, chunk, s_scr[...])
    s_scr[...] = s

    @pl.when(i == pl.num_programs(2) - 1)
    def _():
        sfin_ref[...] = s


def _rwkv_scan(r, k, v, kk, b, lw, s0, tb):
    bsz, l, _ = r.shape
    npair = RW_DIM // LANES
    assert l % tb == 0 and tb % CHUNK == 0
    seq = pl.BlockSpec((None, tb, LANES), lambda bi, p, i: (bi, i, p))
    st = pl.BlockSpec((None, None, LANES, LANES), lambda bi, p, i: (bi, p, 0, 0))
    return pl.pallas_call(
        functools.partial(_rwkv_kernel, nchunks=tb // CHUNK),
        grid=(bsz, npair, l // tb),
        in_specs=[seq] * 6 + [st],
        out_specs=[seq, st],
        out_shape=[jax.ShapeDtypeStruct((bsz, l, RW_DIM), F32),
                   jax.ShapeDtypeStruct((bsz, npair, LANES, LANES), F32)],
        scratch_shapes=[pltpu.VMEM((LANES, LANES), F32)],
        compiler_params=_cparams(("parallel", "parallel", "arbitrary")),
        name="rwkv_scan",
    )(r, k, v, kk, b, lw, s0)


def _ssd_kernel(x_ref, b_ref, c_ref, dt_ref, dtp_ref, arow_ref, apair_ref, s0_ref,
                y_ref, sfin_ref, s_scr, *, nchunks):
    i = pl.program_id(2)

    @pl.when(i == 0)
    def _():
        s_scr[...] = s0_ref[...]

    width = M_HPG * M_HEADDIM
    m0 = lax.broadcasted_iota(jnp.int32, (CHUNK, LANES), 1) < M_HEADDIM
    tr = lax.broadcasted_iota(jnp.int32, (CHUNK, CHUNK), 0)
    tc = lax.broadcasted_iota(jnp.int32, (CHUNK, CHUNK), 1)
    tril = (tr >= tc).astype(F32)
    pr = lax.broadcasted_iota(jnp.int32, (LANES, LANES), 0)
    pc = lax.broadcasted_iota(jnp.int32, (LANES, LANES), 1)
    triu2 = (((pr // CHUNK) == (pc // CHUNK)) & (pr <= pc)).astype(F32)
    er = lax.broadcasted_iota(jnp.int32, (M_HPG, width), 0)
    ec = lax.broadcasted_iota(jnp.int32, (M_HPG, width), 1)
    expand = ((ec // M_HEADDIM) == er).astype(F32)
    lt = lax.broadcasted_iota(jnp.int32, (CHUNK, LANES), 0)
    ll = lax.broadcasted_iota(jnp.int32, (CHUNK, LANES), 1)
    low2 = (ll % CHUNK) <= lt
    arow = arow_ref[...]
    apair = apair_ref[...]

    def chunk(c, s):
        sl = pl.ds(pl.multiple_of(c * CHUNK, CHUNK), CHUNK)
        x = x_ref[sl, :]
        bm = b_ref[sl, :]
        cm = c_ref[sl, :]
        dt = dt_ref[c]
        cum = _dot_f32(tril, dt * arow)
        cum_t = _dot_f32(dtp_ref[c] * apair, triu2)
        cum_x = _dot_f32(cum, expand)
        dt_x = _dot_f32(dt, expand)
        tot_x = cum_x[CHUNK - 1:CHUNK, :]
        xdt = x * dt_x
        xend = xdt * jnp.exp(tot_x - cum_x)
        y_off = _dot(cm, s) * jnp.exp(cum_x)
        cb2 = _dot_nt(cm, jnp.concatenate([bm, bm], axis=0))
        ys = []
        for j in range(M_HPG // 2):
            colsel = jnp.where(m0, cum[:, 2 * j:2 * j + 1], cum[:, 2 * j + 1:2 * j + 2])
            seg = jnp.where(low2, colsel - cum_t[j:j + 1, :], -jnp.inf)
            gp = cb2 * jnp.exp(seg)
            xp = xdt[:, j * LANES:(j + 1) * LANES]
            ys.append(_dot(gp, _stack2(xp, m0)))
        y_ref[sl, :] = y_off + jnp.concatenate(ys, axis=1)
        return s * jnp.exp(tot_x) + _dot_tn(bm, xend)

    s = lax.fori_loop(0, nchunks, chunk, s_scr[...])
    s_scr[...] = s

    @pl.when(i == pl.num_programs(2) - 1)
    def _():
        sfin_ref[...] = s


def _ssd_scan(xs, bm, cm, dt, a, s0, tb):
    bsz, l, _ = xs.shape
    nc = l // CHUNK
    ncb = tb // CHUNK
    width = M_HPG * M_HEADDIM
    dt5 = dt.reshape(bsz, nc, CHUNK, M_GROUPS, M_HPG)
    dt_g = dt5.transpose(0, 3, 1, 2, 4)
    dt_p = (dt5.reshape(bsz, nc, CHUNK, M_GROUPS, M_HPG // 2, 2)
            .transpose(0, 3, 1, 4, 5, 2).reshape(bsz, M_GROUPS, nc, M_HPG // 2, 2 * CHUNK))
    a_row = a.reshape(M_GROUPS, 1, M_HPG)
    a_pair = jnp.repeat(a.reshape(M_GROUPS, M_HPG // 2, 2), CHUNK, axis=-1)
    st = pl.BlockSpec((None, None, M_STATE, width), lambda bi, g, i: (bi, g, 0, 0))
    return pl.pallas_call(
        functools.partial(_ssd_kernel, nchunks=ncb),
        grid=(bsz, M_GROUPS, l // tb),
        in_specs=[pl.BlockSpec((None, tb, width), lambda bi, g, i: (bi, i, g)),
                  pl.BlockSpec((None, tb, M_STATE), lambda bi, g, i: (bi, i, g)),
                  pl.BlockSpec((None, tb, M_STATE), lambda bi, g, i: (bi, i, g)),
                  pl.BlockSpec((None, None, ncb, CHUNK, M_HPG), lambda bi, g, i: (bi, g, i, 0, 0)),
                  pl.BlockSpec((None, None, ncb, M_HPG // 2, 2 * CHUNK),
                               lambda bi, g, i: (bi, g, i, 0, 0)),
                  pl.BlockSpec((None, 1, M_HPG), lambda bi, g, i: (g, 0, 0)),
                  pl.BlockSpec((None, M_HPG // 2, 2 * CHUNK), lambda bi, g, i: (g, 0, 0)),
                  st],
        out_specs=[pl.BlockSpec((None, tb, width), lambda bi, g, i: (bi, i, g)), st],
        out_shape=[jax.ShapeDtypeStruct((bsz, l, M_DIM), F32),
                   jax.ShapeDtypeStruct((bsz, M_GROUPS, M_STATE, width), F32)],
        scratch_shapes=[pltpu.VMEM((M_STATE, width), F32)],
        compiler_params=_cparams(("parallel", "parallel", "arbitrary")),
        name="ssd_scan",
    )(xs, bm, cm, dt_g, dt_p, a_row, a_pair, s0)


def _q_shift(t):
    b, l, ch = t.shape
    rows = l // GRID_W
    q = ch // 4
    g = jnp.pad(t.reshape(b, rows, GRID_W, ch), ((0, 0), (1, 1), (1, 1), (0, 0)))
    left = g[:, 1:-1, :-2, :q]
    right = g[:, 1:-1, 2:, q:2 * q]
    up = g[:, :-2, 1:-1, 2 * q:3 * q]
    down = g[:, 2:, 1:-1, 3 * q:]
    return jnp.concatenate([left, right, up, down], axis=-1).reshape(b, l, ch)


def _bi_shift(t):
    hch = t.shape[-1] // 2
    prev = jnp.pad(t[:, :-1, :hch], ((0, 0), (1, 0), (0, 0)))
    nxt = jnp.pad(t[:, 1:, hch:], ((0, 0), (0, 1), (0, 0)))
    return jnp.concatenate([prev, nxt], axis=-1)


def _mm3(t, w, tm, tn):
    b, l, k = t.shape
    return _mm(t.reshape(b * l, k), w, tm, tn).reshape(b, l, w.shape[1])


def _rwkv_features(u, u_shift, rw_mu, rw_w0, rw_w_up, rw_a0, rw_a_up, rw_k_k, rw_k_a, tm):
    b, l, _ = u.shape
    u = u + (u_shift - u) * rw_mu
    r, k, v = u[..., :RW_DIM], u[..., RW_DIM:2 * RW_DIM], u[..., 2 * RW_DIM:3 * RW_DIM]
    wl = u[..., 3 * RW_DIM:3 * RW_DIM + W_LORA]
    al = u[..., 3 * RW_DIM + W_LORA:3 * RW_DIM + W_LORA + A_LORA]
    gl = u[..., 3 * RW_DIM + W_LORA + A_LORA:]
    tw = jnp.tanh(wl)
    zf = rw_w0[0] + _mm3(tw, rw_w_up[0], tm, RW_DIM)
    zb = rw_w0[1] + _mm3(tw, rw_w_up[1], tm, RW_DIM)
    a = jax.nn.sigmoid(rw_a0 + _mm3(al, rw_a_up, tm, RW_DIM))
    kkh = (k * rw_k_k).reshape(b, l, RW_HEADS, RW_HEAD)
    kkh = kkh / jnp.maximum(jnp.sqrt(jnp.sum(kkh * kkh, axis=-1, keepdims=True)), L2_EPS)
    kk = kkh.reshape(b, l, RW_DIM)
    k = k * (1 + (a - 1) * rw_k_a)
    lw_f = -jnp.exp(-jax.nn.softplus(-zf) - 0.5)
    lw_b = -jnp.exp(-jax.nn.softplus(-zb) - 0.5)
    return r, k, v, kk, kk * a, lw_f, lw_b, gl


def _mamba_features(um, m_conv_w, m_conv_b, m_dt_bias):
    z = um[..., :M_DIM]
    xbc = um[..., M_DIM:M_DIM + M_XBC]
    dtr_f = um[..., M_DIM + M_XBC:M_DIM + M_XBC + M_HEADS]
    dtr_b = um[..., M_DIM + M_XBC + M_HEADS:]
    prev = jnp.pad(xbc[:, :-1], ((0, 0), (1, 0), (0, 0)))
    nxt = jnp.pad(xbc[:, 1:], ((0, 0), (0, 1), (0, 0)))
    xbc = jax.nn.silu(prev * m_conv_w[0] + xbc * m_conv_w[1] + nxt * m_conv_w[2] + m_conv_b)
    xs, bm, cm = xbc[..., :M_DIM], xbc[..., M_DIM:M_DIM + M_BC], xbc[..., M_DIM + M_BC:]
    dt_f = jax.nn.softplus(dtr_f + m_dt_bias[0])
    dt_b = jax.nn.softplus(dtr_b + m_dt_bias[1])
    return z, xs, bm, cm, dt_f, dt_b


def _flip(t):
    return jnp.flip(t, 1)


def kernel(x, c, ctx, c_ctx, ada_w, ada_b, norm_pre, norm_post, ffn1_gate, ffn1_up, ffn1_down,
           ffn2_gate, ffn2_up, ffn2_down, w_in, rw_mu, rw_w0, rw_w_up, rw_a0, rw_a_up, rw_g_up,
           rw_k_k, rw_k_a, rw_r_k, rw_ln_g, rw_ln_b, m_conv_w, m_conv_b, m_dt_bias, m_A_log, m_D,
           m_norm, w_br, w_bm, w_out):
    bsz, l, d = x.shape
    lc = ctx.shape[1]
    i = 0
    tm = 512
    tpb = l // tm

    cc = jnp.concatenate([c, c_ctx[None], jnp.zeros((8 - bsz - 1, d), F32)], axis=0)
    mod = _mm(jax.nn.silu(cc), ada_w[i], 8, 1024) + ada_b[i]
    mod = mod.reshape(8, N_MOD, d)
    lat_row = lambda t: t // tpb
    ctx_row = lambda t: bsz

    bf = lambda w: w.astype(BF16)
    h = x.reshape(bsz * l, d)
    hc = ctx.reshape(bsz * lc, d)
    wg1, wu1, wd1 = bf(ffn1_gate[i]), bf(ffn1_up[i]), bf(ffn1_down[i])
    h = _ffn(h, mod, lat_row, 0, norm_pre[i, 0], norm_post[i, 0], wg1, wu1, wd1, tm, 512)
    hc = _ffn(hc, mod, ctx_row, 0, norm_pre[i, 0], norm_post[i, 0], wg1, wu1, wd1, tm, 512)

    pad_rw = (-RW_COLS) % LANES
    w_cat = jnp.concatenate([w_in[i][:, :RW_COLS], jnp.zeros((d, pad_rw), F32),
                             w_in[i][:, RW_COLS:]], axis=1).astype(BF16)
    o_m = RW_COLS + pad_rw
    o_g = o_m + M_COLS
    tn = 1920
    u = _norm_mm(h, mod, lat_row, 1, norm_pre[i, 1], w_cat, tm, tn).reshape(bsz, l, -1)
    uc = _norm_mm(hc, mod, ctx_row, 1, norm_pre[i, 1], w_cat, tm, tn).reshape(bsz, lc, -1)
    ur, um, ug = u[..., :RW_COLS], u[..., o_m:o_g], u[..., o_g:]
    urc, umc = uc[..., :RW_COLS], uc[..., o_m:o_g]

    feat = functools.partial(_rwkv_features, rw_mu=rw_mu[i], rw_w0=rw_w0[i], rw_w_up=rw_w_up[i],
                             rw_a0=rw_a0[i], rw_a_up=rw_a_up[i], rw_k_k=rw_k_k[i],
                             rw_k_a=rw_k_a[i], tm=tm)
    r, k, v, kk, bvec, lwf, lwb, gl = feat(ur, _q_shift(ur))
    rc, kc, vc, kkc, bc, lwfc, lwbc, _ = feat(urc, _bi_shift(urc))
    s0r = jnp.zeros((bsz, RW_DIM // LANES, LANES, LANES), F32)
    _, src_f = _rwkv_scan(rc, kc, vc, kkc, bc, lwfc, s0r, lc)
    _, src_b = _rwkv_scan(_flip(rc), _flip(kc), _flip(vc), _flip(kkc), _flip(bc), _flip(lwbc),
                          s0r, lc)
    yr_f, _ = _rwkv_scan(r, k, v, kk, bvec, lwf, src_f, tm)
    yr_b, _ = _rwkv_scan(_flip(r), _flip(k), _flip(v), _flip(kk), _flip(bvec), _flip(lwb),
                         src_b, tm)
    yr = (yr_f + _flip(yr_b)).reshape(bsz, l, RW_HEADS, RW_HEAD)
    mu = jnp.mean(yr, axis=-1, keepdims=True)
    var = jnp.mean(jnp.square(yr - mu), axis=-1, keepdims=True)
    yn = ((yr - mu) * lax.rsqrt(var + GN_EPS)).reshape(bsz, l, RW_DIM)
    rkh = (r * k).reshape(bsz, l, RW_HEADS, RW_HEAD) * rw_r_k[i]
    bonus = (jnp.sum(rkh, axis=-1, keepdims=True) * v.reshape(bsz, l, RW_HEADS, RW_HEAD))
    g = _mm3(jax.nn.sigmoid(gl), rw_g_up[i], tm, RW_DIM)
    o_r = (yn * rw_ln_g[i] + rw_ln_b[i] + bonus.reshape(bsz, l, RW_DIM)) * g
    br = _mm3(o_r, bf(w_br[i]), tm, D_MODEL)

    z, xs, bm_, cm_, dtf, dtb = _mamba_features(um, m_conv_w[i], m_conv_b[i], m_dt_bias[i])
    _, xsc, bmc, cmc, dtfc, dtbc = _mamba_features(umc, m_conv_w[i], m_conv_b[i], m_dt_bias[i])
    a_f = -jnp.exp(m_A_log[i, 0])
    a_b = -jnp.exp(m_A_log[i, 1])
    s0m = jnp.zeros((bsz, M_GROUPS, M_STATE, M_HPG * M_HEADDIM), F32)
    _, smc_f = _ssd_scan(xsc, bmc, cmc, dtfc, a_f, s0m, lc)
    _, smc_b = _ssd_scan(_flip(xsc), _flip(bmc), _flip(cmc), _flip(dtbc), a_b, s0m, lc)
    ym_f, _ = _ssd_scan(xs, bm_, cm_, dtf, a_f, smc_f, tm)
    ym_b, _ = _ssd_scan(_flip(xs), _flip(bm_), _flip(cm_), _flip(dtb), a_b, smc_b, tm)
    ym = ym_f + _flip(ym_b) + jnp.repeat(m_D[i], M_HEADDIM) * xs
    ym = ym * jax.nn.silu(z)
    ym = ym * lax.rsqrt(jnp.mean(ym * ym, axis=-1, keepdims=True) + RMS_EPS) * m_norm[i]
    bmo = _mm3(ym, bf(w_bm[i]), tm, D_MODEL)

    merged = jax.nn.sigmoid(ug[..., :D_MODEL]) * br + jax.nn.sigmoid(ug[..., D_MODEL:]) * bmo
    y = _mm3(merged, bf(w_out[i]), tm, D_MODEL).reshape(bsz * l, d)
    h = _resid(h, y, mod, lat_row, 1, norm_post[i, 1], 1.0, tm)
    h = _ffn(h, mod, lat_row, 2, norm_pre[i, 2], norm_post[i, 2], bf(ffn2_gate[i]),
             bf(ffn2_up[i]), bf(ffn2_down[i]), tm, 512)
    return h.reshape(bsz, l, d)
```

```python
import functools
import math

import jax
import jax.numpy as jnp
from jax import lax
from jax.experimental import pallas as pl
from jax.experimental.pallas import tpu as pltpu

F32 = jnp.float32
BF16 = jnp.bfloat16

D_MODEL = 2048
GRID_W = 64
N_MOD = 9
MACARON_W = 0.5
D_FF = 5632
RMS_EPS = 1e-6
GN_EPS = 64e-5
L2_EPS = 1e-12

RW_HEAD = 64
RW_HEADS = D_MODEL // RW_HEAD
RW_DIM = RW_HEADS * RW_HEAD
W_LORA = 96
A_LORA = 96
G_LORA = 256
RW_COLS = 3 * RW_DIM + W_LORA + A_LORA + G_LORA

M_DIM = 2 * D_MODEL
M_HEADDIM = 64
M_HEADS = M_DIM // M_HEADDIM
M_GROUPS = 8
M_HPG = M_HEADS // M_GROUPS
M_STATE = 128
M_BC = M_GROUPS * M_STATE
M_XBC = M_DIM + 2 * M_BC
M_COLS = M_DIM + M_XBC + 2 * M_HEADS
GATE_COLS = 2 * D_MODEL

CHUNK = 64
LANES = 128
SUBLANES = 8
VMEM_LIMIT = 48 * 1024 * 1024

LORA_W = 512
RW_BASE = GATE_COLS
LORA_BASE = RW_BASE + 3 * RW_DIM
Z_BASE = LORA_BASE + LORA_W
XBC_BASE = Z_BASE + M_DIM
DT_BASE = XBC_BASE + M_XBC
U_COLS = DT_BASE + 2 * M_HEADS
FEAT_CW = 256
CONV_CW = 512


def _cparams(sem):
    return pltpu.CompilerParams(dimension_semantics=sem, vmem_limit_bytes=VMEM_LIMIT)


def _dot(a, b):
    return jnp.dot(a.astype(BF16), b.astype(BF16), preferred_element_type=F32)


def _dot_nt(a, b):
    return lax.dot_general(a.astype(BF16), b.astype(BF16), (((1,), (1,)), ((), ())),
                           preferred_element_type=F32)


def _dot_tn(a, b):
    return lax.dot_general(a.astype(BF16), b.astype(BF16), (((0,), (0,)), ((), ())),
                           preferred_element_type=F32)


def _split3(x):
    hi = x.astype(BF16)
    r1 = x - hi.astype(F32)
    mid = r1.astype(BF16)
    lo = (r1 - mid.astype(F32)).astype(BF16)
    return hi, mid, lo


def _dot_01_lhs(t3, x):
    return jnp.dot(t3, jnp.concatenate(_split3(x), axis=0), preferred_element_type=F32)


def _tri3(reverse):
    r = lax.broadcasted_iota(jnp.int32, (CHUNK, 3 * CHUNK), 0)
    c = lax.broadcasted_iota(jnp.int32, (CHUNK, 3 * CHUNK), 1) % CHUNK
    return ((r <= c) if reverse else (r >= c)).astype(BF16)


def _dot_01_rhs(x, e):
    hi, mid, lo = _split3(x)
    return (jnp.dot(hi, e, preferred_element_type=F32) + jnp.dot(mid, e, preferred_element_type=F32)
            + jnp.dot(lo, e, preferred_element_type=F32))


def _group_ones(width, group):
    r = lax.broadcasted_iota(jnp.int32, (width, width), 0)
    c = lax.broadcasted_iota(jnp.int32, (width, width), 1)
    return ((r // group) == (c // group)).astype(BF16)


def _group_sum(x, ones):
    hi = x.astype(BF16)
    lo = (x - hi.astype(F32)).astype(BF16)
    return (jnp.dot(hi, ones, preferred_element_type=F32)
            + jnp.dot(lo, ones, preferred_element_type=F32))


def _sigmoid(x):
    return 1.0 / (1.0 + jnp.exp(-x))


def _softplus(x):
    return jnp.maximum(x, 0.0) + jnp.log(1.0 + jnp.exp(-jnp.abs(x)))


def _mm_kernel(a_ref, b_ref, o_ref):
    o_ref[...] = _dot(a_ref[...], b_ref[...]).astype(o_ref.dtype)


def _mm(a, b, tm, tn, out_dtype=F32):
    m, k = a.shape
    _, n = b.shape
    assert m % tm == 0 and n % tn == 0, (m, n, tm, tn)
    return pl.pallas_call(
        _mm_kernel,
        grid=(m // tm, n // tn),
        in_specs=[pl.BlockSpec((tm, k), lambda i, j: (i, 0)),
                  pl.BlockSpec((k, tn), lambda i, j: (0, j))],
        out_specs=pl.BlockSpec((tm, tn), lambda i, j: (i, j)),
        out_shape=jax.ShapeDtypeStruct((m, n), out_dtype),
        compiler_params=_cparams(("parallel", "arbitrary")),
        name="mm",
    )(a, b)


def _rms(t):
    return t * lax.rsqrt(jnp.mean(t * t, axis=-1, keepdims=True) + RMS_EPS)


def _ffn_kernel(h_ref, mod_ref, gpre_ref, gpost_ref, wg_ref, wu_ref, wd_ref, o_ref,
                xn_ref, acc_ref, *, sub, weight):
    j = pl.program_id(1)

    @pl.when(j == 0)
    def _():
        y = _rms(h_ref[...]) * gpre_ref[...]
        xn = y * (1.0 + mod_ref[0, 3 * sub + 1:3 * sub + 2, :]) + mod_ref[0, 3 * sub:3 * sub + 1, :]
        xn_ref[...] = xn.astype(BF16)
        acc_ref[...] = jnp.zeros_like(acc_ref)

    xn = xn_ref[...]
    g = jnp.dot(xn, wg_ref[...], preferred_element_type=F32)
    u = jnp.dot(xn, wu_ref[...], preferred_element_type=F32)
    a = (g * jax.nn.sigmoid(g)) * u
    acc_ref[...] += jnp.dot(a.astype(BF16), wd_ref[...], preferred_element_type=F32)

    @pl.when(j == pl.num_programs(1) - 1)
    def _():
        yn = _rms(acc_ref[...]) * gpost_ref[...]
        o_ref[...] = h_ref[...] + (weight * mod_ref[0, 3 * sub + 2:3 * sub + 3, :]) * yn


def _ffn(h, mod, mod_row, sub, g_pre, g_post, wg, wu, wd, tm, tf):
    m, d = h.shape
    f = wg.shape[1]
    assert m % tm == 0 and f % tf == 0
    return pl.pallas_call(
        functools.partial(_ffn_kernel, sub=sub, weight=MACARON_W),
        grid=(m // tm, f // tf),
        in_specs=[pl.BlockSpec((tm, d), lambda i, j: (i, 0)),
                  pl.BlockSpec((1, N_MOD, d), lambda i, j: (mod_row(i), 0, 0)),
                  pl.BlockSpec((1, d), lambda i, j: (0, 0)),
                  pl.BlockSpec((1, d), lambda i, j: (0, 0)),
                  pl.BlockSpec((d, tf), lambda i, j: (0, j)),
                  pl.BlockSpec((d, tf), lambda i, j: (0, j)),
                  pl.BlockSpec((tf, d), lambda i, j: (j, 0))],
        out_specs=pl.BlockSpec((tm, d), lambda i, j: (i, 0)),
        out_shape=jax.ShapeDtypeStruct((m, d), F32),
        scratch_shapes=[pltpu.VMEM((tm, d), BF16), pltpu.VMEM((tm, d), F32)],
        compiler_params=_cparams(("parallel", "arbitrary")),
        name="ffn",
    )(h, mod, g_pre.reshape(1, d), g_post.reshape(1, d), wg, wu, wd)


def _norm_mm_kernel(*refs, sub, modulated):
    if modulated:
        h_ref, mod_ref, gpre_ref, w_ref, o_ref, xn_ref = refs
    else:
        h_ref, gpre_ref, w_ref, o_ref, xn_ref = refs

    @pl.when(pl.program_id(1) == 0)
    def _():
        xn = _rms(h_ref[...]) * gpre_ref[...]
        if modulated:
            xn = xn * (1.0 + mod_ref[0, 3 * sub + 1:3 * sub + 2, :]) + mod_ref[0, 3 * sub:3 * sub + 1, :]
        xn_ref[...] = xn.astype(BF16)

    o_ref[...] = jnp.dot(xn_ref[...], w_ref[...], preferred_element_type=F32)


def _norm_mm(h, mod, mod_row, sub, g_pre, w, tm, tn):
    m, d = h.shape
    n = w.shape[1]
    assert m % tm == 0 and n % tn == 0
    modulated = mod is not None
    mod_specs = [pl.BlockSpec((1, N_MOD, d), lambda i, j: (mod_row(i), 0, 0))] if modulated else []
    return pl.pallas_call(
        functools.partial(_norm_mm_kernel, sub=sub, modulated=modulated),
        grid=(m // tm, n // tn),
        in_specs=[pl.BlockSpec((tm, d), lambda i, j: (i, 0))] + mod_specs
        + [pl.BlockSpec((1, d), lambda i, j: (0, 0)),
           pl.BlockSpec((d, tn), lambda i, j: (0, j))],
        out_specs=pl.BlockSpec((tm, tn), lambda i, j: (i, j)),
        out_shape=jax.ShapeDtypeStruct((m, n), F32),
        scratch_shapes=[pltpu.VMEM((tm, d), BF16)],
        compiler_params=_cparams(("parallel", "arbitrary")),
        name="norm_mm",
    )(*([h] + ([mod] if modulated else []) + [g_pre.reshape(1, d), w]))


def _merge_kernel(br_ref, bm_ref, gr_ref, gm_ref, h_ref, mod_ref, gpost_ref, w_ref, o_ref, *, sub):
    merged = _sigmoid(gr_ref[...]) * br_ref[...] + _sigmoid(gm_ref[...]) * bm_ref[...]
    y = jnp.dot(merged.astype(BF16), w_ref[...], preferred_element_type=F32)
    yn = _rms(y) * gpost_ref[...]
    o_ref[...] = h_ref[...] + mod_ref[0, 3 * sub + 2:3 * sub + 3, :] * yn


def _merge(br, bm, u2d, h, mod, mod_row, sub, g_post, w, tm):
    m, d = h.shape
    tile = pl.BlockSpec((tm, d), lambda i: (i, 0))
    return pl.pallas_call(
        functools.partial(_merge_kernel, sub=sub),
        grid=(m // tm,),
        in_specs=[tile, tile,
                  pl.BlockSpec((tm, d), lambda i: (i, 0)),
                  pl.BlockSpec((tm, d), lambda i: (i, 1)),
                  tile,
                  pl.BlockSpec((1, N_MOD, d), lambda i: (mod_row(i), 0, 0)),
                  pl.BlockSpec((1, d), lambda i: (0, 0)),
                  pl.BlockSpec((d, d), lambda i: (0, 0))],
        out_specs=tile,
        out_shape=jax.ShapeDtypeStruct((m, d), F32),
        compiler_params=_cparams(("parallel",)),
        name="merge",
    )(br, bm, u2d, u2d, h, mod, g_post.reshape(1, d), w)


def _rwkv_feat_kernel(*refs, mode, period, tiles_per_seq, tt):
    if mode == "grid":
        (r_ref, k_ref, v_ref, lo_ref, kprev_ref, vprev_ref, vnext_ref, lonext_ref,
         mur_ref, muk_ref, muv_ref, mulo_ref, wl_ref, pv_ref) = refs[:14]
        rest = refs[14:]
    else:
        (r_ref, k_ref, v_ref, lo_ref, mur_ref, muk_ref, muv_ref, mulo_ref, wl_ref, pv_ref) = refs[:10]
        rest = refs[10:]
    r_o, k_o, v_o, kk_o, b_o, lwf_o, lwb_o, g_o, bonus_o, act_scr = rest
    i = pl.program_id(0)
    j = pl.program_id(1)
    cw = r_ref.shape[1]
    t = lax.broadcasted_iota(jnp.int32, (tt, 1), 0)
    row_first = (t % period) == 0
    row_last = (t % period) == period - 1
    seq_first = (i % tiles_per_seq) == 0
    seq_last = (i % tiles_per_seq) == tiles_per_seq - 1
    quarter = RW_COLS // 4

    def prev1(x):
        return jnp.where(row_first, 0.0, pltpu.roll(x, 1, 0))

    def next1(x):
        return jnp.where(row_last, 0.0, pltpu.roll(x, tt - 1, 0))

    def up(x, halo_ref):
        halo = jnp.where(seq_first, 0.0, halo_ref[...])
        return jnp.concatenate([halo, x[:tt - GRID_W]], axis=0)

    def down(x, halo_ref):
        halo = jnp.where(seq_last, 0.0, halo_ref[...])
        return jnp.concatenate([x[GRID_W:], halo], axis=0)

    def col_index(base, width):
        return base + lax.broadcasted_iota(jnp.int32, (1, width), 1)

    def lerp(x, xs, mu_ref):
        return x + (xs - x) * mu_ref[...]

    @pl.when(j == 0)
    def _():
        x = lo_ref[...]
        xs = down(x, lonext_ref) if mode == "grid" else next1(x)
        x = lerp(x, xs, mulo_ref)
        c = col_index(0, LORA_W)
        act = jnp.where(c < W_LORA, jnp.tanh(x), jnp.where(c < W_LORA + A_LORA, x, _sigmoid(x)))
        act_scr[...] = act.astype(BF16)

    x = r_ref[...]
    c = col_index(j * cw, cw)
    if mode == "grid":
        xs = jnp.where(c < quarter, prev1(x), next1(x))
    else:
        xs = prev1(x)
    r = lerp(x, xs, mur_ref)

    x = k_ref[...]
    c = col_index(RW_DIM + j * cw, cw)
    if mode == "grid":
        xs = jnp.where(c < 2 * quarter, next1(x), up(x, kprev_ref))
    else:
        xs = jnp.where(c < 2 * quarter, prev1(x), next1(x))
    k = lerp(x, xs, muk_ref)

    x = v_ref[...]
    c = col_index(2 * RW_DIM + j * cw, cw)
    if mode == "grid":
        xs = jnp.where(c < 3 * quarter, up(x, vprev_ref), down(x, vnext_ref))
    else:
        xs = next1(x)
    v = lerp(x, xs, muv_ref)

    lo = jnp.dot(act_scr[...], wl_ref[...], preferred_element_type=F32)
    w0f, w0b, a0 = pv_ref[0:1, :], pv_ref[1:2, :], pv_ref[2:3, :]
    k_k, k_a, r_k = pv_ref[3:4, :], pv_ref[4:5, :], pv_ref[5:6, :]
    a = _sigmoid(a0 + lo[:, 2 * cw:3 * cw])
    ones = _group_ones(cw, RW_HEAD)
    kkr = k * k_k
    nrm = jnp.sqrt(_group_sum(kkr * kkr, ones))
    kk = kkr / jnp.maximum(nrm, L2_EPS)
    k = k * (1.0 + (a - 1.0) * k_a)
    lwf_o[...] = -math.exp(-0.5) * _sigmoid(w0f + lo[:, 0:cw])
    lwb_o[...] = -math.exp(-0.5) * _sigmoid(w0b + lo[:, cw:2 * cw])
    r_o[...] = r
    k_o[...] = k
    v_o[...] = v
    kk_o[...] = kk
    b_o[...] = kk * a
    g_o[...] = lo[:, 3 * cw:4 * cw]
    bonus_o[...] = _group_sum(r * k * r_k, ones) * v


def _rwkv_feat(u2d, mode, seq_len, tt, mu, w_lora, pv):
    m = u2d.shape[0]
    cw = FEAT_CW
    ncb = RW_DIM // cw
    assert m % tt == 0 and seq_len % tt == 0
    tiles_per_seq = seq_len // tt
    rb, kb, vb = RW_BASE // cw, (RW_BASE + RW_DIM) // cw, (RW_BASE + 2 * RW_DIM) // cw
    lob = LORA_BASE // LORA_W
    hr = tt // GRID_W
    nhb = m // GRID_W
    main = [pl.BlockSpec((tt, cw), lambda i, j: (i, rb + j)),
            pl.BlockSpec((tt, cw), lambda i, j: (i, kb + j)),
            pl.BlockSpec((tt, cw), lambda i, j: (i, vb + j)),
            pl.BlockSpec((tt, LORA_W), lambda i, j: (i, lob))]
    args = [u2d, u2d, u2d, u2d]
    if mode == "grid":
        assert tt % GRID_W == 0
        period = GRID_W
        prev_row = lambda i: jnp.maximum(i * hr - 1, 0)
        next_row = lambda i: jnp.minimum((i + 1) * hr, nhb - 1)
        main += [pl.BlockSpec((GRID_W, cw), lambda i, j: (prev_row(i), kb + j)),
                 pl.BlockSpec((GRID_W, cw), lambda i, j: (prev_row(i), vb + j)),
                 pl.BlockSpec((GRID_W, cw), lambda i, j: (next_row(i), vb + j)),
                 pl.BlockSpec((GRID_W, LORA_W), lambda i, j: (next_row(i), lob))]
        args += [u2d, u2d, u2d, u2d]
    else:
        assert tiles_per_seq == 1
        period = tt
    params = [pl.BlockSpec((1, cw), lambda i, j: (0, j)),
              pl.BlockSpec((1, cw), lambda i, j: (0, RW_DIM // cw + j)),
              pl.BlockSpec((1, cw), lambda i, j: (0, 2 * RW_DIM // cw + j)),
              pl.BlockSpec((1, LORA_W), lambda i, j: (0, 3 * RW_DIM // LORA_W)),
              pl.BlockSpec((None, LORA_W, 4 * cw), lambda i, j: (j, 0, 0)),
              pl.BlockSpec((SUBLANES, cw), lambda i, j: (0, j))]
    args += [mu, mu, mu, mu, w_lora, pv]
    out_spec = pl.BlockSpec((tt, cw), lambda i, j: (i, j))
    return pl.pallas_call(
        functools.partial(_rwkv_feat_kernel, mode=mode, period=period,
                          tiles_per_seq=tiles_per_seq, tt=tt),
        grid=(m // tt, ncb),
        in_specs=main + params,
        out_specs=[out_spec] * 9,
        out_shape=[jax.ShapeDtypeStruct((m, RW_DIM), F32)] * 9,
        scratch_shapes=[pltpu.VMEM((tt, LORA_W), BF16)],
        compiler_params=_cparams(("parallel", "arbitrary")),
        name="rwkv_feat",
    )(*args)


def _conv_kernel(x_ref, prev_ref, next_ref, w_ref, o_ref, *, tiles_per_seq, tt):
    i = pl.program_id(0)
    x = x_ref[...]
    t = lax.broadcasted_iota(jnp.int32, (tt, 1), 0)
    seq_first = (i % tiles_per_seq) == 0
    seq_last = (i % tiles_per_seq) == tiles_per_seq - 1
    hp = jnp.where(seq_first, 0.0, prev_ref[SUBLANES - 1:SUBLANES, :])
    hn = jnp.where(seq_last, 0.0, next_ref[0:1, :])
    prev = jnp.where(t == 0, hp, pltpu.roll(x, 1, 0))
    nxt = jnp.where(t == tt - 1, hn, pltpu.roll(x, tt - 1, 0))
    y = prev * w_ref[0:1, :] + x * w_ref[1:2, :] + nxt * w_ref[2:3, :] + w_ref[3:4, :]
    o_ref[...] = y * _sigmoid(y)


def _conv(u2d, seq_len, tt, wconv):
    m = u2d.shape[0]
    cw = CONV_CW
    base = XBC_BASE // cw
    tiles_per_seq = seq_len // tt
    hr = tt // SUBLANES
    nhb = m // SUBLANES
    return pl.pallas_call(
        functools.partial(_conv_kernel, tiles_per_seq=tiles_per_seq, tt=tt),
        grid=(m // tt, M_XBC // cw),
        in_specs=[pl.BlockSpec((tt, cw), lambda i, j: (i, base + j)),
                  pl.BlockSpec((SUBLANES, cw), lambda i, j: (jnp.maximum(i * hr - 1, 0), base + j)),
                  pl.BlockSpec((SUBLANES, cw),
                               lambda i, j: (jnp.minimum((i + 1) * hr, nhb - 1), base + j)),
                  pl.BlockSpec((SUBLANES, cw), lambda i, j: (0, j))],
        out_specs=pl.BlockSpec((tt, cw), lambda i, j: (i, j)),
        out_shape=jax.ShapeDtypeStruct((m, M_XBC), F32),
        compiler_params=_cparams(("parallel", "parallel")),
        name="conv",
    )(u2d, u2d, u2d, wconv)


def _stack2(x, m0):
    return jnp.concatenate([jnp.where(m0, x, 0.0), jnp.where(m0, 0.0, x)], axis=0)


def _unstack2(x2):
    return x2[:CHUNK] + x2[CHUNK:]


def _rwkv_kernel(*refs, nchunks, reverse, emit, finish):
    r_ref, k_ref, v_ref, kk_ref, b_ref, lw_ref, s0_ref = refs[:7]
    rest = list(refs[7:])
    if finish:
        yacc_ref, bonus_ref, g_ref, lng_ref, lnb_ref = rest[:5]
        rest = rest[5:]
    y_ref = rest.pop(0) if emit else None
    sfin_ref, s_scr = rest
    i = pl.program_id(2)

    @pl.when(i == 0)
    def _():
        s_scr[...] = s0_ref[...]

    m0 = lax.broadcasted_iota(jnp.int32, (CHUNK, LANES), 1) < RW_HEAD
    row = lax.broadcasted_iota(jnp.int32, (2 * CHUNK, LANES), 0)
    col = lax.broadcasted_iota(jnp.int32, (2 * CHUNK, LANES), 1)
    same = (row // CHUNK) == (col // RW_HEAD)
    rt = row % CHUNK
    cs = col % RW_HEAD
    if reverse:
        strict = same & (cs > rt)
        incl = same & (cs >= rt)
    else:
        strict = same & (cs < rt)
        incl = same & (cs <= rt)
    eye = (row == col).astype(F32)
    tri3 = _tri3(reverse)
    last = 0 if reverse else CHUNK - 1

    G = range(nchunks)
    sls = [slice(g * CHUNK, (g + 1) * CHUNK) for g in G]
    lw = [lw_ref[sl, :] for sl in sls]
    cum = [_dot_01_lhs(tri3, x) for x in lw]
    tot = [c[last:last + 1, :] for c in cum]
    eneg = [jnp.exp(-c) for c in cum]
    a2 = [_stack2(-kk_ref[sl, :] * jnp.exp(c - x), m0).astype(BF16)
          for sl, c, x in zip(sls, cum, lw)]
    bt = [(b_ref[sl, :] * e).astype(BF16) for sl, e in zip(sls, eneg)]
    kt = [(k_ref[sl, :] * e).astype(BF16) for sl, e in zip(sls, eneg)]
    bb = [jnp.concatenate([x, x], axis=0) for x in bt]
    kb = [jnp.concatenate([x, x], axis=0) for x in kt]
    n = [jnp.where(strict, _dot_nt(x, y), 0.0) for x, y in zip(a2, bb)]
    ak = [jnp.where(strict, _dot_nt(x, y), 0.0).astype(BF16) for x, y in zip(a2, kb)]
    v2 = [_stack2(v_ref[sl, :], m0).astype(BF16) for sl in sls]
    eend = [jnp.exp(t - c) for t, c in zip(tot, cum)]
    bg2 = [_stack2(b_ref[sl, :] * e, m0).astype(BF16) for sl, e in zip(sls, eend)]
    kg2 = [_stack2(k_ref[sl, :] * e, m0).astype(BF16) for sl, e in zip(sls, eend)]
    if emit:
        r2 = [_stack2(r_ref[sl, :] * jnp.exp(c), m0) for sl, c in zip(sls, cum)]
        r2b = [x.astype(BF16) for x in r2]
        rb = [jnp.where(incl, _dot_nt(x, y), 0.0).astype(BF16) for x, y in zip(r2b, bb)]
        rk = [jnp.where(incl, _dot_nt(x, y), 0.0).astype(BF16) for x, y in zip(r2b, kb)]
    akv = [_dot(x, y).astype(BF16) for x, y in zip(ak, v2)]
    p = [eye + x for x in n]
    npow = [x.astype(BF16) for x in n]
    for it in range(5):
        sq = [_dot(x, x) for x in npow]
        npow = [x.astype(BF16) for x in sq]
        p = [x + _dot(x.astype(BF16), y) for x, y in zip(p, npow)]
    wu = [_dot(x.astype(BF16), jnp.concatenate([y, z], axis=1))
          for x, y, z in zip(p, a2, akv)]
    w2 = [x[:, :LANES].astype(BF16) for x in wu]
    up2 = [x[:, LANES:].astype(BF16) for x in wu]
    uv = [jnp.concatenate([x, y], axis=0) for x, y in zip(up2, v2)]
    mc = [_dot_tn(x, y).astype(BF16) for x, y in zip(w2, bg2)]
    nc = [_dot_tn(x, jnp.concatenate([y, z], axis=0)) for x, y, z in zip(uv, bg2, kg2)]
    gam = [jnp.exp(t) for t in tot]
    if emit:
        q = [_unstack2(x + _dot(y, z)).astype(BF16) for x, y, z in zip(r2, rb, w2)]
        yp = [_unstack2(_dot(jnp.concatenate([x, y], axis=1), z)) for x, y, z in zip(rb, rk, uv)]

    s = s_scr[...]
    ys = [None] * nchunks
    for g in (reversed(G) if reverse else G):
        sb = s.astype(BF16)
        if emit:
            ys[g] = _dot_nt(q[g], sb) + yp[g]
        s = s * gam[g] + _dot(sb, mc[g]) + nc[g]
    s_scr[...] = s

    if emit and not finish:
        for g in G:
            y_ref[sls[g], :] = ys[g]
    if finish:
        ones = _group_ones(LANES, RW_HEAD)
        ysum = [y + yacc_ref[sl, :] for y, sl in zip(ys, sls)]
        mean = [_group_sum(y, ones) * (1.0 / RW_HEAD) for y in ysum]
        dev = [y - mu for y, mu in zip(ysum, mean)]
        var = [_group_sum(d * d, ones) * (1.0 / RW_HEAD) for d in dev]
        for g in G:
            yn = dev[g] * lax.rsqrt(var[g] + GN_EPS)
            o = (yn * lng_ref[...] + lnb_ref[...] + bonus_ref[sls[g], :]) * g_ref[sls[g], :]
            y_ref[sls[g], :] = o.astype(y_ref.dtype)

    @pl.when(i == pl.num_programs(2) - 1)
    def _():
        sfin_ref[...] = s


def _rwkv_scan(r, k, v, kk, b, lw, s0, tb, reverse, emit=True, finish=None):
    bsz, l, _ = r.shape
    npair = RW_DIM // LANES
    nblk = l // tb
    assert l % tb == 0 and tb % CHUNK == 0
    blk = (lambda i: nblk - 1 - i) if reverse else (lambda i: i)
    seq = pl.BlockSpec((None, tb, LANES), lambda bi, p, i: (bi, blk(i), p))
    st = pl.BlockSpec((None, None, LANES, LANES), lambda bi, p, i: (bi, p, 0, 0))
    vec = pl.BlockSpec((1, LANES), lambda bi, p, i: (0, p))
    st_shape = jax.ShapeDtypeStruct((bsz, npair, LANES, LANES), F32)
    y_shape = jax.ShapeDtypeStruct((bsz, l, RW_DIM), BF16 if finish else F32)
    args = [r, k, v, kk, b, lw, s0]
    in_specs = [seq] * 6 + [st]
    if finish:
        y_other, bonus, g, ln_g, ln_b = finish
        args += [y_other, bonus, g, ln_g.reshape(1, RW_DIM), ln_b.reshape(1, RW_DIM)]
        in_specs += [seq, seq, seq, vec, vec]
    outs = pl.pallas_call(
        functools.partial(_rwkv_kernel, nchunks=tb // CHUNK, reverse=reverse, emit=emit,
                          finish=bool(finish)),
        grid=(bsz, npair, nblk),
        in_specs=in_specs,
        out_specs=([seq] if emit else []) + [st],
        out_shape=([y_shape] if emit else []) + [st_shape],
        scratch_shapes=[pltpu.VMEM((LANES, LANES), F32)],
        compiler_params=_cparams(("parallel", "parallel", "arbitrary")),
        name="rwkv_scan",
    )(*args)
    return outs if emit else (None, outs[0])


def _ssd_kernel(*refs, nchunks, reverse, emit, finish):
    x_ref, b_ref, c_ref, dt_ref, dtp_ref, arow_ref, apair_ref, s0_ref = refs[:8]
    rest = list(refs[8:])
    if finish:
        yacc_ref, z_ref, dskip_ref = rest[:3]
        rest = rest[3:]
    y_ref = rest.pop(0) if emit else None
    sfin_ref, s_scr = rest
    i = pl.program_id(2)

    @pl.when(i == 0)
    def _():
        s_scr[...] = s0_ref[...]

    width = M_HPG * M_HEADDIM
    m0 = lax.broadcasted_iota(jnp.int32, (CHUNK, LANES), 1) < M_HEADDIM
    tri3 = _tri3(reverse)
    pr = lax.broadcasted_iota(jnp.int32, (LANES, LANES), 0)
    pc = lax.broadcasted_iota(jnp.int32, (LANES, LANES), 1)
    tri_t = (((pr // CHUNK) == (pc // CHUNK)) & ((pr >= pc) if reverse else (pr <= pc))).astype(BF16)
    er = lax.broadcasted_iota(jnp.int32, (M_HPG, width), 0)
    ec = lax.broadcasted_iota(jnp.int32, (M_HPG, width), 1)
    expand = ((ec // M_HEADDIM) == er).astype(BF16)
    lt = lax.broadcasted_iota(jnp.int32, (CHUNK, LANES), 0)
    ll = lax.broadcasted_iota(jnp.int32, (CHUNK, LANES), 1)
    keep = ((ll % CHUNK) >= lt) if reverse else ((ll % CHUNK) <= lt)
    arow = arow_ref[...]
    apair = apair_ref[...]
    last = 0 if reverse else CHUNK - 1

    G = range(nchunks)
    sls = [slice(g * CHUNK, (g + 1) * CHUNK) for g in G]
    dt = [dt_ref[g] for g in G]
    cum = [_dot_01_lhs(tri3, x * arow) for x in dt]
    cum_x = [_dot_01_rhs(x, expand) for x in cum]
    dt_x = [_dot_01_rhs(x, expand) for x in dt]
    tot_x = [x[last:last + 1, :] for x in cum_x]
    xdt = [x_ref[sl, :] * y for sl, y in zip(sls, dt_x)]
    xend = [x * jnp.exp(t - c) for x, t, c in zip(xdt, tot_x, cum_x)]
    bm = [b_ref[sl, :].astype(BF16) for sl in sls]
    cs = [_dot_tn(x, y) for x, y in zip(bm, xend)]
    gam = [jnp.exp(t) for t in tot_x]
    if emit:
        cm = [c_ref[sl, :].astype(BF16) for sl in sls]
        cum_t = [_dot_01_rhs(dtp_ref[g] * apair, tri_t) for g in G]
        cb2 = [_dot_nt(x, jnp.concatenate([y, y], axis=0)) for x, y in zip(cm, bm)]
        ydiag = []
        for g in G:
            ys = []
            for j in range(M_HPG // 2):
                colsel = jnp.where(m0, cum[g][:, 2 * j:2 * j + 1], cum[g][:, 2 * j + 1:2 * j + 2])
                seg = jnp.where(keep, colsel - cum_t[g][j:j + 1, :], -jnp.inf)
                gp = cb2[g] * jnp.exp(seg)
                xp = xdt[g][:, j * LANES:(j + 1) * LANES]
                ys.append(_dot(gp, _stack2(xp, m0)))
            ydiag.append(jnp.concatenate(ys, axis=1))
        ecum = [jnp.exp(x) for x in cum_x]

    s = s_scr[...]
    for g in (reversed(G) if reverse else G):
        if emit:
            y = _dot(cm[g], s) * ecum[g] + ydiag[g]
            if finish:
                zz = z_ref[sls[g], :]
                y = (y + yacc_ref[sls[g], :] + dskip_ref[...] * x_ref[sls[g], :]) * (zz * _sigmoid(zz))
            y_ref[sls[g], :] = y
        s = s * gam[g] + cs[g]
    s_scr[...] = s

    @pl.when(i == pl.num_programs(2) - 1)
    def _():
        sfin_ref[...] = s


def _ssd_scan(xbc, dt, a, s0, tb, reverse, emit=True, finish=None):
    bsz, l, _ = xbc.shape
    nc = l // CHUNK
    ncb = tb // CHUNK
    nblk = l // tb
    width = M_HPG * M_HEADDIM
    dt5 = dt.reshape(bsz, nc, CHUNK, M_GROUPS, M_HPG)
    dt_g = dt5.transpose(0, 3, 1, 2, 4)
    dt_p = (dt5.reshape(bsz, nc, CHUNK, M_GROUPS, M_HPG // 2, 2)
            .transpose(0, 3, 1, 4, 5, 2).reshape(bsz, M_GROUPS, nc, M_HPG // 2, 2 * CHUNK))
    a_row = a.reshape(M_GROUPS, 1, M_HPG)
    a_pair = jnp.repeat(a.reshape(M_GROUPS, M_HPG // 2, 2), CHUNK, axis=-1)
    blk = (lambda i: nblk - 1 - i) if reverse else (lambda i: i)
    st = pl.BlockSpec((None, None, M_STATE, width), lambda bi, g, i: (bi, g, 0, 0))
    xspec = pl.BlockSpec((None, tb, width), lambda bi, g, i: (bi, blk(i), g))
    bspec = pl.BlockSpec((None, tb, M_STATE), lambda bi, g, i: (bi, blk(i), M_DIM // M_STATE + g))
    cspec = pl.BlockSpec((None, tb, M_STATE),
                         lambda bi, g, i: (bi, blk(i), (M_DIM + M_BC) // M_STATE + g))
    args = [xbc, xbc, xbc, dt_g, dt_p, a_row, a_pair, s0]
    in_specs = [xspec, bspec, cspec,
                pl.BlockSpec((None, None, ncb, CHUNK, M_HPG),
                             lambda bi, g, i: (bi, g, blk(i), 0, 0)),
                pl.BlockSpec((None, None, ncb, M_HPG // 2, 2 * CHUNK),
                             lambda bi, g, i: (bi, g, blk(i), 0, 0)),
                pl.BlockSpec((None, 1, M_HPG), lambda bi, g, i: (g, 0, 0)),
                pl.BlockSpec((None, M_HPG // 2, 2 * CHUNK), lambda bi, g, i: (g, 0, 0)),
                st]
    aliases = {}
    if finish:
        y_other, u, dskip = finish
        args += [y_other, u, dskip]
        in_specs += [xspec,
                     pl.BlockSpec((None, tb, width), lambda bi, g, i: (bi, blk(i), Z_BASE // width + g)),
                     pl.BlockSpec((1, width), lambda bi, g, i: (0, g))]
        aliases = {8: 0}
    outs = pl.pallas_call(
        functools.partial(_ssd_kernel, nchunks=ncb, reverse=reverse, emit=emit,
                          finish=bool(finish)),
        grid=(bsz, M_GROUPS, nblk),
        in_specs=in_specs,
        out_specs=([xspec] if emit else []) + [st],
        out_shape=([jax.ShapeDtypeStruct((bsz, l, M_DIM), F32)] if emit else [])
        + [jax.ShapeDtypeStruct((bsz, M_GROUPS, M_STATE, width), F32)],
        scratch_shapes=[pltpu.VMEM((M_STATE, width), F32)],
        input_output_aliases=aliases,
        compiler_params=_cparams(("parallel", "parallel", "arbitrary")),
        name="ssd_scan",
    )(*args)
    return outs if emit else (None, outs[0])


def _dt_features(u, m_dt_bias):
    dtr = u[..., DT_BASE:]
    return (jax.nn.softplus(dtr[..., :M_HEADS] + m_dt_bias[0]),
            jax.nn.softplus(dtr[..., M_HEADS:] + m_dt_bias[1]))


def kernel(x, c, ctx, c_ctx, ada_w, ada_b, norm_pre, norm_post, ffn1_gate, ffn1_up, ffn1_down,
           ffn2_gate, ffn2_up, ffn2_down, w_in, rw_mu, rw_w0, rw_w_up, rw_a0, rw_a_up, rw_g_up,
           rw_k_k, rw_k_a, rw_r_k, rw_ln_g, rw_ln_b, m_conv_w, m_conv_b, m_dt_bias, m_A_log, m_D,
           m_norm, w_br, w_bm, w_out):
    bsz, l, d = x.shape
    lc = ctx.shape[1]
    i = 0
    tm = 512
    tpb = l // tm

    cc = jnp.concatenate([c, c_ctx[None], jnp.zeros((SUBLANES - bsz - 1, d), F32)], axis=0)
    mod = _mm(jax.nn.silu(cc), ada_w[i], SUBLANES, 1024) + ada_b[i]
    mod = mod.reshape(SUBLANES, N_MOD, d)
    lat_row = lambda t: t // tpb
    ctx_row = lambda t: bsz

    bf = lambda w: w.astype(BF16)
    h = x.reshape(bsz * l, d)
    hc = ctx.reshape(bsz * lc, d)
    wg1, wu1, wd1 = bf(ffn1_gate[i]), bf(ffn1_up[i]), bf(ffn1_down[i])
    h = _ffn(h, mod, lat_row, 0, norm_pre[i, 0], norm_post[i, 0], wg1, wu1, wd1, tm, 512)
    hc = _ffn(hc, mod, ctx_row, 0, norm_pre[i, 0], norm_post[i, 0], wg1, wu1, wd1, tm, 512)

    wi = w_in[i]
    o_m = RW_COLS
    o_g = RW_COLS + M_COLS
    w_cat = jnp.concatenate([wi[:, o_g:], wi[:, :RW_COLS],
                             jnp.zeros((d, LORA_W - (RW_COLS - 3 * RW_DIM)), F32),
                             wi[:, o_m:o_m + M_DIM + M_XBC], wi[:, o_m + M_DIM + M_XBC:o_g]],
                            axis=1).astype(BF16)
    tn = 1920
    u2d = _norm_mm(h, mod, lat_row, 1, norm_pre[i, 1], w_cat, tm, tn)
    uc2d = _norm_mm(hc, mod, ctx_row, 1, norm_pre[i, 1], w_cat, tm, tn)
    u = u2d.reshape(bsz, l, U_COLS)
    uc = uc2d.reshape(bsz, lc, U_COLS)

    cw = FEAT_CW
    mu = jnp.pad(rw_mu[i], (0, 3 * RW_DIM + LORA_W - RW_COLS)).reshape(1, -1)
    zero = lambda rows: jnp.zeros((rows, RW_DIM), F32)
    wl_rows = lambda w, before, after: jnp.concatenate([zero(before), w, zero(after)], axis=0)
    w_lora = jnp.stack([wl_rows(rw_w_up[i, 0], 0, LORA_W - W_LORA),
                        wl_rows(rw_w_up[i, 1], 0, LORA_W - W_LORA),
                        wl_rows(rw_a_up[i], W_LORA, LORA_W - W_LORA - A_LORA),
                        wl_rows(rw_g_up[i], W_LORA + A_LORA, LORA_W - W_LORA - A_LORA - G_LORA)],
                       axis=1)
    w_lora = (w_lora.reshape(LORA_W, 4, RW_DIM // cw, cw).transpose(2, 0, 1, 3)
              .reshape(RW_DIM // cw, LORA_W, 4 * cw).astype(BF16))
    pv = jnp.stack([rw_w0[i, 0], rw_w0[i, 1], rw_a0[i], rw_k_k[i], rw_k_a[i],
                    rw_r_k[i].reshape(RW_DIM), jnp.zeros((RW_DIM,), F32),
                    jnp.zeros((RW_DIM,), F32)], axis=0)
    seq3 = lambda t, n: t.reshape(bsz, n, RW_DIM)
    r, k, v, kk, bvec, lwf, lwb, g, bonus = [
        seq3(t, l) for t in _rwkv_feat(u2d, "grid", l, tm, mu, w_lora, pv)]
    rc, kc, vc, kkc, bc, lwfc, lwbc, _, _ = [
        seq3(t, lc) for t in _rwkv_feat(uc2d, "seq", lc, lc, mu, w_lora, pv)]
    s0r = jnp.zeros((bsz, RW_DIM // LANES, LANES, LANES), F32)
    _, src_f = _rwkv_scan(rc, kc, vc, kkc, bc, lwfc, s0r, lc, False, emit=False)
    _, src_b = _rwkv_scan(rc, kc, vc, kkc, bc, lwbc, s0r, lc, True, emit=False)
    yr, _ = _rwkv_scan(r, k, v, kk, bvec, lwf, src_f, tm, False)
    o_r, _ = _rwkv_scan(r, k, v, kk, bvec, lwb, src_b, tm, True,
                        finish=(yr, bonus, g, rw_ln_g[i], rw_ln_b[i]))
    br = _mm(o_r.reshape(bsz * l, RW_DIM), bf(w_br[i]), tm, D_MODEL)

    wconv = jnp.concatenate([m_conv_w[i], m_conv_b[i][None],
                             jnp.zeros((SUBLANES - 4, M_XBC), F32)], axis=0)
    xbc = _conv(u2d, l, tm, wconv).reshape(bsz, l, M_XBC)
    xbcc = _conv(uc2d, lc, lc, wconv).reshape(bsz, lc, M_XBC)
    dtf, dtb = _dt_features(u, m_dt_bias[i])
    dtfc, dtbc = _dt_features(uc, m_dt_bias[i])
    a_f = -jnp.exp(m_A_log[i, 0])
    a_b = -jnp.exp(m_A_log[i, 1])
    s0m = jnp.zeros((bsz, M_GROUPS, M_STATE, M_HPG * M_HEADDIM), F32)
    _, smc_f = _ssd_scan(xbcc, dtfc, a_f, s0m, lc, False, emit=False)
    _, smc_b = _ssd_scan(xbcc, dtbc, a_b, s0m, lc, True, emit=False)
    ym, _ = _ssd_scan(xbc, dtf, a_f, smc_f, tm, False)
    dskip = jnp.repeat(m_D[i], M_HEADDIM).reshape(1, M_DIM)
    ym, _ = _ssd_scan(xbc, dtb, a_b, smc_b, tm, True, finish=(ym, u, dskip))
    bmo = _norm_mm(ym.reshape(bsz * l, M_DIM), None, None, 0, m_norm[i], bf(w_bm[i]), tm, 512)

    h = _merge(br, bmo, u2d, h, mod, lambda t: t // (l // 256), 1, norm_post[i, 1], bf(w_out[i]), 256)
    h = _ffn(h, mod, lat_row, 2, norm_pre[i, 2], norm_post[i, 2], bf(ffn2_gate[i]),
             bf(ffn2_up[i]), bf(ffn2_down[i]), tm, 512)
    return h.reshape(bsz, l, d)
```

```python
import functools
import math

import jax
import jax.numpy as jnp
from jax import lax
from jax.experimental import pallas as pl
from jax.experimental.pallas import tpu as pltpu

F32 = jnp.float32
BF16 = jnp.bfloat16

D_MODEL = 2048
GRID_W = 64
N_MOD = 9
MACARON_W = 0.5
D_FF = 5632
RMS_EPS = 1e-6
GN_EPS = 64e-5
L2_EPS = 1e-12

RW_HEAD = 64
RW_HEADS = D_MODEL // RW_HEAD
RW_DIM = RW_HEADS * RW_HEAD
W_LORA = 96
A_LORA = 96
G_LORA = 256
RW_COLS = 3 * RW_DIM + W_LORA + A_LORA + G_LORA

M_DIM = 2 * D_MODEL
M_HEADDIM = 64
M_HEADS = M_DIM // M_HEADDIM
M_GROUPS = 8
M_HPG = M_HEADS // M_GROUPS
M_STATE = 128
M_BC = M_GROUPS * M_STATE
M_XBC = M_DIM + 2 * M_BC
M_COLS = M_DIM + M_XBC + 2 * M_HEADS
GATE_COLS = 2 * D_MODEL

CHUNK = 64
LANES = 128
SUBLANES = 8
VMEM_LIMIT = 48 * 1024 * 1024

LORA_W = 512
RW_BASE = GATE_COLS
LORA_BASE = RW_BASE + 3 * RW_DIM
Z_BASE = LORA_BASE + LORA_W
XBC_BASE = Z_BASE + M_DIM
DT_BASE = XBC_BASE + M_XBC
U_COLS = DT_BASE + 2 * M_HEADS
FEAT_CW = 256
CONV_CW = 512


def _cparams(sem):
    return pltpu.CompilerParams(dimension_semantics=sem, vmem_limit_bytes=VMEM_LIMIT)


def _dot(a, b):
    return jnp.dot(a.astype(BF16), b.astype(BF16), preferred_element_type=F32)


def _dot_nt(a, b):
    return lax.dot_general(a.astype(BF16), b.astype(BF16), (((1,), (1,)), ((), ())),
                           preferred_element_type=F32)


def _dot_tn(a, b):
    return lax.dot_general(a.astype(BF16), b.astype(BF16), (((0,), (0,)), ((), ())),
                           preferred_element_type=F32)


def _split3(x):
    hi = x.astype(BF16)
    r1 = x - hi.astype(F32)
    mid = r1.astype(BF16)
    lo = (r1 - mid.astype(F32)).astype(BF16)
    return hi, mid, lo


def _dot_01_lhs(t3, x):
    return jnp.dot(t3, jnp.concatenate(_split3(x), axis=0), preferred_element_type=F32)


def _tri3(reverse):
    r = lax.broadcasted_iota(jnp.int32, (CHUNK, 3 * CHUNK), 0)
    c = lax.broadcasted_iota(jnp.int32, (CHUNK, 3 * CHUNK), 1) % CHUNK
    return ((r <= c) if reverse else (r >= c)).astype(BF16)


def _dot_01_rhs(x, e):
    hi, mid, lo = _split3(x)
    return (jnp.dot(hi, e, preferred_element_type=F32) + jnp.dot(mid, e, preferred_element_type=F32)
            + jnp.dot(lo, e, preferred_element_type=F32))


def _group_ones(width, group):
    r = lax.broadcasted_iota(jnp.int32, (width, width), 0)
    c = lax.broadcasted_iota(jnp.int32, (width, width), 1)
    return ((r // group) == (c // group)).astype(BF16)


def _group_sum(x, ones):
    hi = x.astype(BF16)
    lo = (x - hi.astype(F32)).astype(BF16)
    return (jnp.dot(hi, ones, preferred_element_type=F32)
            + jnp.dot(lo, ones, preferred_element_type=F32))


def _sigmoid(x):
    return 1.0 / (1.0 + jnp.exp(-x))


def _softplus(x):
    return jnp.maximum(x, 0.0) + jnp.log(1.0 + jnp.exp(-jnp.abs(x)))


def _mm_kernel(a_ref, b_ref, o_ref):
    o_ref[...] = _dot(a_ref[...], b_ref[...]).astype(o_ref.dtype)


def _mm(a, b, tm, tn, out_dtype=F32):
    m, k = a.shape
    _, n = b.shape
    assert m % tm == 0 and n % tn == 0, (m, n, tm, tn)
    return pl.pallas_call(
        _mm_kernel,
        grid=(m // tm, n // tn),
        in_specs=[pl.BlockSpec((tm, k), lambda i, j: (i, 0)),
                  pl.BlockSpec((k, tn), lambda i, j: (0, j))],
        out_specs=pl.BlockSpec((tm, tn), lambda i, j: (i, j)),
        out_shape=jax.ShapeDtypeStruct((m, n), out_dtype),
        compiler_params=_cparams(("parallel", "arbitrary")),
        name="mm",
    )(a, b)


def _rms(t):
    return t * lax.rsqrt(jnp.mean(t * t, axis=-1, keepdims=True) + RMS_EPS)


def _ffn_kernel(h_ref, mod_ref, gpre_ref, gpost_ref, wg_ref, wu_ref, wd_ref, o_ref,
                xn_ref, acc_ref, *, sub, weight):
    j = pl.program_id(1)

    @pl.when(j == 0)
    def _():
        y = _rms(h_ref[...]) * gpre_ref[...]
        xn = y * (1.0 + mod_ref[0, 3 * sub + 1:3 * sub + 2, :]) + mod_ref[0, 3 * sub:3 * sub + 1, :]
        xn_ref[...] = xn.astype(BF16)
        acc_ref[...] = jnp.zeros_like(acc_ref)

    xn = xn_ref[...]
    g = jnp.dot(xn, wg_ref[...], preferred_element_type=F32)
    u = jnp.dot(xn, wu_ref[...], preferred_element_type=F32)
    a = (g * jax.nn.sigmoid(g)) * u
    acc_ref[...] += jnp.dot(a.astype(BF16), wd_ref[...], preferred_element_type=F32)

    @pl.when(j == pl.num_programs(1) - 1)
    def _():
        yn = _rms(acc_ref[...]) * gpost_ref[...]
        o_ref[...] = h_ref[...] + (weight * mod_ref[0, 3 * sub + 2:3 * sub + 3, :]) * yn


def _ffn(h, mod, mod_row, sub, g_pre, g_post, wg, wu, wd, tm, tf):
    m, d = h.shape
    f = wg.shape[1]
    assert m % tm == 0 and f % tf == 0
    return pl.pallas_call(
        functools.partial(_ffn_kernel, sub=sub, weight=MACARON_W),
        grid=(m // tm, f // tf),
        in_specs=[pl.BlockSpec((tm, d), lambda i, j: (i, 0)),
                  pl.BlockSpec((1, N_MOD, d), lambda i, j: (mod_row(i), 0, 0)),
                  pl.BlockSpec((1, d), lambda i, j: (0, 0)),
                  pl.BlockSpec((1, d), lambda i, j: (0, 0)),
                  pl.BlockSpec((d, tf), lambda i, j: (0, j)),
                  pl.BlockSpec((d, tf), lambda i, j: (0, j)),
                  pl.BlockSpec((tf, d), lambda i, j: (j, 0))],
        out_specs=pl.BlockSpec((tm, d), lambda i, j: (i, 0)),
        out_shape=jax.ShapeDtypeStruct((m, d), F32),
        scratch_shapes=[pltpu.VMEM((tm, d), BF16), pltpu.VMEM((tm, d), F32)],
        compiler_params=_cparams(("parallel", "arbitrary")),
        name="ffn",
    )(h, mod, g_pre.reshape(1, d), g_post.reshape(1, d), wg, wu, wd)


def _norm_mm_kernel(*refs, sub, modulated):
    if modulated:
        h_ref, mod_ref, gpre_ref, w_ref, o_ref, xn_ref = refs
    else:
        h_ref, gpre_ref, w_ref, o_ref, xn_ref = refs

    @pl.when(pl.program_id(1) == 0)
    def _():
        xn = _rms(h_ref[...]) * gpre_ref[...]
        if modulated:
            xn = xn * (1.0 + mod_ref[0, 3 * sub + 1:3 * sub + 2, :]) + mod_ref[0, 3 * sub:3 * sub + 1, :]
        xn_ref[...] = xn.astype(BF16)

    o_ref[...] = jnp.dot(xn_ref[...], w_ref[...], preferred_element_type=F32)


def _norm_mm(h, mod, mod_row, sub, g_pre, w, tm, tn):
    m, d = h.shape
    n = w.shape[1]
    assert m % tm == 0 and n % tn == 0
    modulated = mod is not None
    mod_specs = [pl.BlockSpec((1, N_MOD, d), lambda i, j: (mod_row(i), 0, 0))] if modulated else []
    return pl.pallas_call(
        functools.partial(_norm_mm_kernel, sub=sub, modulated=modulated),
        grid=(m // tm, n // tn),
        in_specs=[pl.BlockSpec((tm, d), lambda i, j: (i, 0))] + mod_specs
        + [pl.BlockSpec((1, d), lambda i, j: (0, 0)),
           pl.BlockSpec((d, tn), lambda i, j: (0, j))],
        out_specs=pl.BlockSpec((tm, tn), lambda i, j: (i, j)),
        out_shape=jax.ShapeDtypeStruct((m, n), F32),
        scratch_shapes=[pltpu.VMEM((tm, d), BF16)],
        compiler_params=_cparams(("parallel", "arbitrary")),
        name="norm_mm",
    )(*([h] + ([mod] if modulated else []) + [g_pre.reshape(1, d), w]))


def _merge_kernel(br_ref, bm_ref, gr_ref, gm_ref, h_ref, mod_ref, gpost_ref, w_ref, o_ref, *, sub):
    merged = _sigmoid(gr_ref[...]) * br_ref[...] + _sigmoid(gm_ref[...]) * bm_ref[...]
    y = jnp.dot(merged.astype(BF16), w_ref[...], preferred_element_type=F32)
    yn = _rms(y) * gpost_ref[...]
    o_ref[...] = h_ref[...] + mod_ref[0, 3 * sub + 2:3 * sub + 3, :] * yn


def _merge(br, bm, u2d, h, mod, mod_row, sub, g_post, w, tm):
    m, d = h.shape
    tile = pl.BlockSpec((tm, d), lambda i: (i, 0))
    return pl.pallas_call(
        functools.partial(_merge_kernel, sub=sub),
        grid=(m // tm,),
        in_specs=[tile, tile,
                  pl.BlockSpec((tm, d), lambda i: (i, 0)),
                  pl.BlockSpec((tm, d), lambda i: (i, 1)),
                  tile,
                  pl.BlockSpec((1, N_MOD, d), lambda i: (mod_row(i), 0, 0)),
                  pl.BlockSpec((1, d), lambda i: (0, 0)),
                  pl.BlockSpec((d, d), lambda i: (0, 0))],
        out_specs=tile,
        out_shape=jax.ShapeDtypeStruct((m, d), F32),
        compiler_params=_cparams(("parallel",)),
        name="merge",
    )(br, bm, u2d, u2d, h, mod, g_post.reshape(1, d), w)


def _rwkv_feat_kernel(*refs, mode, period, tiles_per_seq, tt):
    if mode == "grid":
        (r_ref, k_ref, v_ref, lo_ref, kprev_ref, vprev_ref, vnext_ref, lonext_ref,
         mur_ref, muk_ref, muv_ref, mulo_ref, wl_ref, pv_ref) = refs[:14]
        rest = refs[14:]
    else:
        (r_ref, k_ref, v_ref, lo_ref, mur_ref, muk_ref, muv_ref, mulo_ref, wl_ref, pv_ref) = refs[:10]
        rest = refs[10:]
    r_o, k_o, v_o, kk_o, b_o, lwf_o, lwb_o, g_o, bonus_o, act_scr = rest
    i = pl.program_id(0)
    j = pl.program_id(1)
    cw = r_ref.shape[1]
    t = lax.broadcasted_iota(jnp.int32, (tt, 1), 0)
    row_first = (t % period) == 0
    row_last = (t % period) == period - 1
    seq_first = (i % tiles_per_seq) == 0
    seq_last = (i % tiles_per_seq) == tiles_per_seq - 1
    quarter = RW_COLS // 4

    def prev1(x):
        return jnp.where(row_first, 0.0, pltpu.roll(x, 1, 0))

    def next1(x):
        return jnp.where(row_last, 0.0, pltpu.roll(x, tt - 1, 0))

    def up(x, halo_ref):
        halo = jnp.where(seq_first, 0.0, halo_ref[...])
        return jnp.concatenate([halo, x[:tt - GRID_W]], axis=0)

    def down(x, halo_ref):
        halo = jnp.where(seq_last, 0.0, halo_ref[...])
        return jnp.concatenate([x[GRID_W:], halo], axis=0)

    def col_index(base, width):
        return base + lax.broadcasted_iota(jnp.int32, (1, width), 1)

    def lerp(x, xs, mu_ref):
        return x + (xs - x) * mu_ref[...]

    @pl.when(j == 0)
    def _():
        x = lo_ref[...]
        xs = down(x, lonext_ref) if mode == "grid" else next1(x)
        x = lerp(x, xs, mulo_ref)
        c = col_index(0, LORA_W)
        act = jnp.where(c < W_LORA, jnp.tanh(x), jnp.where(c < W_LORA + A_LORA, x, _sigmoid(x)))
        act_scr[...] = act.astype(BF16)

    x = r_ref[...]
    c = col_index(j * cw, cw)
    if mode == "grid":
        xs = jnp.where(c < quarter, prev1(x), next1(x))
    else:
        xs = prev1(x)
    r = lerp(x, xs, mur_ref)

    x = k_ref[...]
    c = col_index(RW_DIM + j * cw, cw)
    if mode == "grid":
        xs = jnp.where(c < 2 * quarter, next1(x), up(x, kprev_ref))
    else:
        xs = jnp.where(c < 2 * quarter, prev1(x), next1(x))
    k = lerp(x, xs, muk_ref)

    x = v_ref[...]
    c = col_index(2 * RW_DIM + j * cw, cw)
    if mode == "grid":
        xs = jnp.where(c < 3 * quarter, up(x, vprev_ref), down(x, vnext_ref))
    else:
        xs = next1(x)
    v = lerp(x, xs, muv_ref)

    lo = jnp.dot(act_scr[...], wl_ref[...], preferred_element_type=F32)
    w0f, w0b, a0 = pv_ref[0:1, :], pv_ref[1:2, :], pv_ref[2:3, :]
    k_k, k_a, r_k = pv_ref[3:4, :], pv_ref[4:5, :], pv_ref[5:6, :]
    a = _sigmoid(a0 + lo[:, 2 * cw:3 * cw])
    ones = _group_ones(cw, RW_HEAD)
    kkr = k * k_k
    nrm = jnp.sqrt(_group_sum(kkr * kkr, ones))
    kk = kkr / jnp.maximum(nrm, L2_EPS)
    k = k * (1.0 + (a - 1.0) * k_a)
    lwf_o[...] = -math.exp(-0.5) * _sigmoid(w0f + lo[:, 0:cw])
    lwb_o[...] = -math.exp(-0.5) * _sigmoid(w0b + lo[:, cw:2 * cw])
    r_o[...] = r
    k_o[...] = k
    v_o[...] = v
    kk_o[...] = kk
    b_o[...] = kk * a
    g_o[...] = lo[:, 3 * cw:4 * cw]
    bonus_o[...] = _group_sum(r * k * r_k, ones) * v


def _rwkv_feat(u2d, mode, seq_len, tt, mu, w_lora, pv):
    m = u2d.shape[0]
    cw = FEAT_CW
    ncb = RW_DIM // cw
    assert m % tt == 0 and seq_len % tt == 0
    tiles_per_seq = seq_len // tt
    rb, kb, vb = RW_BASE // cw, (RW_BASE + RW_DIM) // cw, (RW_BASE + 2 * RW_DIM) // cw
    lob = LORA_BASE // LORA_W
    hr = tt // GRID_W
    nhb = m // GRID_W
    main = [pl.BlockSpec((tt, cw), lambda i, j: (i, rb + j)),
            pl.BlockSpec((tt, cw), lambda i, j: (i, kb + j)),
            pl.BlockSpec((tt, cw), lambda i, j: (i, vb + j)),
            pl.BlockSpec((tt, LORA_W), lambda i, j: (i, lob))]
    args = [u2d, u2d, u2d, u2d]
    if mode == "grid":
        assert tt % GRID_W == 0
        period = GRID_W
        prev_row = lambda i: jnp.maximum(i * hr - 1, 0)
        next_row = lambda i: jnp.minimum((i + 1) * hr, nhb - 1)
        main += [pl.BlockSpec((GRID_W, cw), lambda i, j: (prev_row(i), kb + j)),
                 pl.BlockSpec((GRID_W, cw), lambda i, j: (prev_row(i), vb + j)),
                 pl.BlockSpec((GRID_W, cw), lambda i, j: (next_row(i), vb + j)),
                 pl.BlockSpec((GRID_W, LORA_W), lambda i, j: (next_row(i), lob))]
        args += [u2d, u2d, u2d, u2d]
    else:
        assert tiles_per_seq == 1
        period = tt
    params = [pl.BlockSpec((1, cw), lambda i, j: (0, j)),
              pl.BlockSpec((1, cw), lambda i, j: (0, RW_DIM // cw + j)),
              pl.BlockSpec((1, cw), lambda i, j: (0, 2 * RW_DIM // cw + j)),
              pl.BlockSpec((1, LORA_W), lambda i, j: (0, 3 * RW_DIM // LORA_W)),
              pl.BlockSpec((None, LORA_W, 4 * cw), lambda i, j: (j, 0, 0)),
              pl.BlockSpec((SUBLANES, cw), lambda i, j: (0, j))]
    args += [mu, mu, mu, mu, w_lora, pv]
    out_spec = pl.BlockSpec((tt, cw), lambda i, j: (i, j))
    return pl.pallas_call(
        functools.partial(_rwkv_feat_kernel, mode=mode, period=period,
                          tiles_per_seq=tiles_per_seq, tt=tt),
        grid=(m // tt, ncb),
        in_specs=main + params,
        out_specs=[out_spec] * 9,
        out_shape=[jax.ShapeDtypeStruct((m, RW_DIM), F32)] * 9,
        scratch_shapes=[pltpu.VMEM((tt, LORA_W), BF16)],
        compiler_params=_cparams(("parallel", "arbitrary")),
        name="rwkv_feat",
    )(*args)


def _conv_kernel(x_ref, prev_ref, next_ref, w_ref, o_ref, *, tiles_per_seq, tt):
    i = pl.program_id(0)
    x = x_ref[...]
    t = lax.broadcasted_iota(jnp.int32, (tt, 1), 0)
    seq_first = (i % tiles_per_seq) == 0
    seq_last = (i % tiles_per_seq) == tiles_per_seq - 1
    hp = jnp.where(seq_first, 0.0, prev_ref[SUBLANES - 1:SUBLANES, :])
    hn = jnp.where(seq_last, 0.0, next_ref[0:1, :])
    prev = jnp.where(t == 0, hp, pltpu.roll(x, 1, 0))
    nxt = jnp.where(t == tt - 1, hn, pltpu.roll(x, tt - 1, 0))
    y = prev * w_ref[0:1, :] + x * w_ref[1:2, :] + nxt * w_ref[2:3, :] + w_ref[3:4, :]
    o_ref[...] = y * _sigmoid(y)


def _conv(u2d, seq_len, tt, wconv):
    m = u2d.shape[0]
    cw = CONV_CW
    base = XBC_BASE // cw
    tiles_per_seq = seq_len // tt
    hr = tt // SUBLANES
    nhb = m // SUBLANES
    return pl.pallas_call(
        functools.partial(_conv_kernel, tiles_per_seq=tiles_per_seq, tt=tt),
        grid=(m // tt, M_XBC // cw),
        in_specs=[pl.BlockSpec((tt, cw), lambda i, j: (i, base + j)),
                  pl.BlockSpec((SUBLANES, cw), lambda i, j: (jnp.maximum(i * hr - 1, 0), base + j)),
                  pl.BlockSpec((SUBLANES, cw),
                               lambda i, j: (jnp.minimum((i + 1) * hr, nhb - 1), base + j)),
                  pl.BlockSpec((SUBLANES, cw), lambda i, j: (0, j))],
        out_specs=pl.BlockSpec((tt, cw), lambda i, j: (i, j)),
        out_shape=jax.ShapeDtypeStruct((m, M_XBC), F32),
        compiler_params=_cparams(("parallel", "parallel")),
        name="conv",
    )(u2d, u2d, u2d, wconv)


def _stack2(x, m0):
    return jnp.concatenate([jnp.where(m0, x, 0.0), jnp.where(m0, 0.0, x)], axis=0)


def _rwkv_kernel(*refs, nchunks, pp, reverse, emit, finish):
    r_ref, k_ref, v_ref, kk_ref, b_ref, lw_ref, s0_ref = refs[:7]
    rest = list(refs[7:])
    if finish:
        yacc_ref, bonus_ref, g_ref, lng_ref, lnb_ref = rest[:5]
        rest = rest[5:]
    y_ref = rest.pop(0) if emit else None
    sfin_ref, s_scr = rest
    i = pl.program_id(2)

    @pl.when(i == 0)
    def _():
        s_scr[...] = s0_ref[...]

    lane = lax.broadcasted_iota(jnp.int32, (CHUNK, LANES), 1)
    trow = lax.broadcasted_iota(jnp.int32, (CHUNK, LANES), 0)
    m0 = lane < RW_HEAD
    scol = lane % RW_HEAD
    strict = (scol > trow) if reverse else (scol < trow)
    incl = (scol >= trow) if reverse else (scol <= trow)
    eye = (scol == trow).astype(F32)
    row2 = lax.broadcasted_iota(jnp.int32, (LANES, LANES), 0)
    col2 = lax.broadcasted_iota(jnp.int32, (LANES, LANES), 1)
    same = (row2 // RW_HEAD) == (col2 // RW_HEAD)
    tri3 = _tri3(reverse)
    last = 0 if reverse else CHUNK - 1
    bf = lambda x: x.astype(BF16)
    st2 = lambda x: _stack2(x, m0)

    sls = [(slice(c * CHUNK, (c + 1) * CHUNK), slice(hp * LANES, (hp + 1) * LANES))
           for c in range(nchunks) for hp in range(pp)]
    G = range(nchunks * pp)
    lw = [lw_ref[sl] for sl in sls]
    cum = [_dot_01_lhs(tri3, x) for x in lw]
    tot = [c[last:last + 1, :] for c in cum]
    eneg = [jnp.exp(-c) for c in cum]
    at = [bf(-kk_ref[sl] * jnp.exp(c - x)) for sl, c, x in zip(sls, cum, lw)]
    bk2 = [jnp.concatenate([st2(bf(b_ref[sl] * e)), st2(bf(k_ref[sl] * e))], axis=0)
           for sl, e in zip(sls, eneg)]
    if emit:
        rt = [r_ref[sl] * jnp.exp(c) for sl, c in zip(sls, cum)]
        big = [_dot_nt(jnp.concatenate([x, bf(y)], axis=0), z) for x, y, z in zip(at, rt, bk2)]
        rb = [bf(jnp.where(incl, x[CHUNK:, :LANES], 0.0)) for x in big]
        rk = [bf(jnp.where(incl, x[CHUNK:, LANES:], 0.0)) for x in big]
    else:
        big = [_dot_nt(x, z) for x, z in zip(at, bk2)]
    n = [jnp.where(strict, x[:CHUNK, :LANES], 0.0) for x in big]
    ak = [bf(jnp.where(strict, x[:CHUNK, LANES:], 0.0)) for x in big]
    v = [bf(v_ref[sl]) for sl in sls]
    v2 = [st2(x) for x in v]
    akv = [bf(_dot(x, y)) for x, y in zip(ak, v2)]
    p = [eye + x for x in n]
    npow = [bf(x) for x in n]
    npow = [bf(_dot(x, st2(x))) for x in npow]
    for it in range(4):
        both = [_dot(jnp.concatenate([x, bf(y)], axis=0), st2(x)) for x, y in zip(npow, p)]
        p = [y + x[CHUNK:] for x, y in zip(both, p)]
        npow = [bf(x[:CHUNK]) for x in both]
    p = [bf(y + _dot(bf(y), st2(x))) for x, y in zip(npow, p)]
    wu = [_dot(x, jnp.concatenate([st2(y), st2(z)], axis=1)) for x, y, z in zip(p, at, akv)]
    w = [bf(x[:, :LANES]) for x in wu]
    up = [bf(x[:, LANES:]) for x in wu]
    eend = [jnp.exp(t - c) for t, c in zip(tot, cum)]
    bg = [bf(b_ref[sl] * e) for sl, e in zip(sls, eend)]
    kg = [bf(k_ref[sl] * e) for sl, e in zip(sls, eend)]
    mc = [bf(jnp.where(same, _dot_tn(x, y), 0.0)) for x, y in zip(w, bg)]
    nc = [jnp.where(same, _dot_tn(jnp.concatenate([x, y], axis=0),
                                  jnp.concatenate([z, t], axis=0)), 0.0)
          for x, y, z, t in zip(up, v, bg, kg)]
    gam = [jnp.exp(t) for t in tot]
    if emit:
        q = [bf(x + _dot(y, st2(z))) for x, y, z in zip(rt, rb, w)]
        yp = [_dot(jnp.concatenate([x, y], axis=1), jnp.concatenate([st2(z), t], axis=0))
              for x, y, z, t in zip(rb, rk, up, v2)]

    s = [s_scr[hp] for hp in range(pp)]
    ys = [None] * len(G)
    for c in (reversed(range(nchunks)) if reverse else range(nchunks)):
        for hp in range(pp):
            g = c * pp + hp
            sb = s[hp].astype(BF16)
            if emit:
                ys[g] = _dot_nt(q[g], sb) + yp[g]
            s[hp] = s[hp] * gam[g] + _dot(sb, mc[g]) + nc[g]
    for hp in range(pp):
        s_scr[hp] = s[hp]

    if emit and not finish:
        for g in G:
            y_ref[sls[g]] = ys[g]
    if finish:
        ones = _group_ones(LANES, RW_HEAD)
        ysum = [y + yacc_ref[sl] for y, sl in zip(ys, sls)]
        mean = [_group_sum(y, ones) * (1.0 / RW_HEAD) for y in ysum]
        dev = [y - mu for y, mu in zip(ysum, mean)]
        var = [_group_sum(d * d, ones) * (1.0 / RW_HEAD) for d in dev]
        for g in G:
            yn = dev[g] * lax.rsqrt(var[g] + GN_EPS)
            lanes = sls[g][1]
            o = (yn * lng_ref[:, lanes] + lnb_ref[:, lanes] + bonus_ref[sls[g]]) * g_ref[sls[g]]
            y_ref[sls[g]] = o.astype(y_ref.dtype)

    @pl.when(i == pl.num_programs(2) - 1)
    def _():
        for hp in range(pp):
            sfin_ref[hp] = s[hp]


RW_PAIRS_PER_STEP = 4


def _rwkv_scan(r, k, v, kk, b, lw, s0, tb, reverse, emit=True, finish=None):
    bsz, l, _ = r.shape
    pp = RW_PAIRS_PER_STEP
    npair = RW_DIM // LANES
    nblk = l // tb
    assert l % tb == 0 and tb % CHUNK == 0 and npair % pp == 0
    blk = (lambda i: nblk - 1 - i) if reverse else (lambda i: i)
    seq = pl.BlockSpec((None, tb, pp * LANES), lambda bi, p, i: (bi, blk(i), p))
    st = pl.BlockSpec((None, pp, LANES, LANES), lambda bi, p, i: (bi, p, 0, 0))
    vec = pl.BlockSpec((1, pp * LANES), lambda bi, p, i: (0, p))
    st_shape = jax.ShapeDtypeStruct((bsz, npair, LANES, LANES), F32)
    y_shape = jax.ShapeDtypeStruct((bsz, l, RW_DIM), BF16 if finish else F32)
    args = [r, k, v, kk, b, lw, s0]
    in_specs = [seq] * 6 + [st]
    if finish:
        y_other, bonus, g, ln_g, ln_b = finish
        args += [y_other, bonus, g, ln_g.reshape(1, RW_DIM), ln_b.reshape(1, RW_DIM)]
        in_specs += [seq, seq, seq, vec, vec]
    outs = pl.pallas_call(
        functools.partial(_rwkv_kernel, nchunks=tb // CHUNK, pp=pp, reverse=reverse, emit=emit,
                          finish=bool(finish)),
        grid=(bsz, npair // pp, nblk),
        in_specs=in_specs,
        out_specs=([seq] if emit else []) + [st],
        out_shape=([y_shape] if emit else []) + [st_shape],
        scratch_shapes=[pltpu.VMEM((pp, LANES, LANES), F32)],
        compiler_params=_cparams(("parallel", "parallel", "arbitrary")),
        name="rwkv_scan",
    )(*args)
    return outs if emit else (None, outs[0])


def _ssd_kernel(*refs, nchunks, reverse, emit, finish):
    x_ref, b_ref, c_ref, dt_ref, dtp_ref, arow_ref, apair_ref, s0_ref = refs[:8]
    rest = list(refs[8:])
    if finish:
        yacc_ref, z_ref, dskip_ref = rest[:3]
        rest = rest[3:]
    y_ref = rest.pop(0) if emit else None
    sfin_ref, s_scr = rest
    i = pl.program_id(2)

    @pl.when(i == 0)
    def _():
        s_scr[...] = s0_ref[...]

    width = M_HPG * M_HEADDIM
    m0 = lax.broadcasted_iota(jnp.int32, (CHUNK, LANES), 1) < M_HEADDIM
    tri3 = _tri3(reverse)
    pr = lax.broadcasted_iota(jnp.int32, (LANES, LANES), 0)
    pc = lax.broadcasted_iota(jnp.int32, (LANES, LANES), 1)
    tri_t = (((pr // CHUNK) == (pc // CHUNK)) & ((pr >= pc) if reverse else (pr <= pc))).astype(BF16)
    er = lax.broadcasted_iota(jnp.int32, (3 * M_HPG, width), 0)
    ec = lax.broadcasted_iota(jnp.int32, (3 * M_HPG, width), 1)
    expand3 = ((ec // M_HEADDIM) == (er % M_HPG)).astype(BF16)
    lt = lax.broadcasted_iota(jnp.int32, (CHUNK, LANES), 0)
    ll = lax.broadcasted_iota(jnp.int32, (CHUNK, LANES), 1)
    keep = ((ll % CHUNK) >= lt) if reverse else ((ll % CHUNK) <= lt)
    a_x = arow_ref[...]
    apair = apair_ref[...]
    last = 0 if reverse else CHUNK - 1

    G = range(nchunks)
    sls = [slice(g * CHUNK, (g + 1) * CHUNK) for g in G]
    dt_x = [jnp.dot(jnp.concatenate(_split3(dt_ref[g]), axis=1), expand3,
                    preferred_element_type=F32) for g in G]
    cum_x = [_dot_01_lhs(tri3, x * a_x) for x in dt_x]
    tot_x = [x[last:last + 1, :] for x in cum_x]
    xdt = [x_ref[sl] * y for sl, y in zip(sls, dt_x)]
    xend = [x * jnp.exp(t - c) for x, t, c in zip(xdt, tot_x, cum_x)]
    bm = [b_ref[sl].astype(BF16) for sl in sls]
    cs = [_dot_tn(x, y) for x, y in zip(bm, xend)]
    gam = [jnp.exp(t) for t in tot_x]
    if emit:
        cm = [c_ref[sl].astype(BF16) for sl in sls]
        cum_t = [_dot_01_rhs(dtp_ref[g] * apair, tri_t) for g in G]
        cb2 = [_dot_nt(x, jnp.concatenate([y, y], axis=0)) for x, y in zip(cm, bm)]
        ydiag = []
        for g in G:
            ys = []
            for j in range(M_HPG // 2):
                colsel = cum_x[g][:, j * LANES:(j + 1) * LANES]
                seg = jnp.where(keep, colsel - cum_t[g][j:j + 1, :], -jnp.inf)
                gp = cb2[g] * jnp.exp(seg)
                xp = xdt[g][:, j * LANES:(j + 1) * LANES]
                ys.append(_dot(gp, _stack2(xp, m0)))
            ydiag.append(jnp.concatenate(ys, axis=1))
        ecum = [jnp.exp(x) for x in cum_x]

    s = s_scr[...]
    for g in (reversed(G) if reverse else G):
        if emit:
            y = _dot(cm[g], s) * ecum[g] + ydiag[g]
            if finish:
                zz = z_ref[sls[g]]
                y = (y + yacc_ref[sls[g]] + dskip_ref[...] * x_ref[sls[g]]) * (zz * _sigmoid(zz))
            y_ref[sls[g]] = y
        s = s * gam[g] + cs[g]
    s_scr[...] = s

    @pl.when(i == pl.num_programs(2) - 1)
    def _():
        sfin_ref[...] = s


def _ssd_scan(xbc, dt, a, s0, tb, reverse, emit=True, finish=None):
    bsz, l, _ = xbc.shape
    nc = l // CHUNK
    ncb = tb // CHUNK
    nblk = l // tb
    width = M_HPG * M_HEADDIM
    dt5 = dt.reshape(bsz, nc, CHUNK, M_GROUPS, M_HPG)
    dt_g = dt5.transpose(0, 3, 1, 2, 4)
    dt_p = (dt5.reshape(bsz, nc, CHUNK, M_GROUPS, M_HPG // 2, 2)
            .transpose(0, 3, 1, 4, 5, 2).reshape(bsz, M_GROUPS, nc, M_HPG // 2, 2 * CHUNK))
    a_row = jnp.repeat(a, M_HEADDIM).reshape(M_GROUPS, 1, width)
    a_pair = jnp.repeat(a.reshape(M_GROUPS, M_HPG // 2, 2), CHUNK, axis=-1)
    blk = (lambda i: nblk - 1 - i) if reverse else (lambda i: i)
    st = pl.BlockSpec((None, None, M_STATE, width), lambda bi, g, i: (bi, g, 0, 0))
    xspec = pl.BlockSpec((None, tb, width), lambda bi, g, i: (bi, blk(i), g))
    bspec = pl.BlockSpec((None, tb, M_STATE), lambda bi, g, i: (bi, blk(i), M_DIM // M_STATE + g))
    cspec = pl.BlockSpec((None, tb, M_STATE),
                         lambda bi, g, i: (bi, blk(i), (M_DIM + M_BC) // M_STATE + g))
    args = [xbc, xbc, xbc, dt_g, dt_p, a_row, a_pair, s0]
    in_specs = [xspec, bspec, cspec,
                pl.BlockSpec((None, None, ncb, CHUNK, M_HPG),
                             lambda bi, g, i: (bi, g, blk(i), 0, 0)),
                pl.BlockSpec((None, None, ncb, M_HPG // 2, 2 * CHUNK),
                             lambda bi, g, i: (bi, g, blk(i), 0, 0)),
                pl.BlockSpec((None, 1, width), lambda bi, g, i: (g, 0, 0)),
                pl.BlockSpec((None, M_HPG // 2, 2 * CHUNK), lambda bi, g, i: (g, 0, 0)),
                st]
    aliases = {}
    if finish:
        y_other, u, dskip = finish
        args += [y_other, u, dskip]
        in_specs += [xspec,
                     pl.BlockSpec((None, tb, width), lambda bi, g, i: (bi, blk(i), Z_BASE // width + g)),
                     pl.BlockSpec((1, width), lambda bi, g, i: (0, g))]
        aliases = {8: 0}
    outs = pl.pallas_call(
        functools.partial(_ssd_kernel, nchunks=ncb, reverse=reverse, emit=emit,
                          finish=bool(finish)),
        grid=(bsz, M_GROUPS, nblk),
        in_specs=in_specs,
        out_specs=([xspec] if emit else []) + [st],
        out_shape=([jax.ShapeDtypeStruct((bsz, l, M_DIM), F32)] if emit else [])
        + [jax.ShapeDtypeStruct((bsz, M_GROUPS, M_STATE, width), F32)],
        scratch_shapes=[pltpu.VMEM((M_STATE, width), F32)],
        input_output_aliases=aliases,
        compiler_params=_cparams(("parallel", "parallel", "arbitrary")),
        name="ssd_scan",
    )(*args)
    return outs if emit else (None, outs[0])


def _dt_features(u, m_dt_bias):
    dtr = u[..., DT_BASE:]
    return (jax.nn.softplus(dtr[..., :M_HEADS] + m_dt_bias[0]),
            jax.nn.softplus(dtr[..., M_HEADS:] + m_dt_bias[1]))


def kernel(x, c, ctx, c_ctx, ada_w, ada_b, norm_pre, norm_post, ffn1_gate, ffn1_up, ffn1_down,
           ffn2_gate, ffn2_up, ffn2_down, w_in, rw_mu, rw_w0, rw_w_up, rw_a0, rw_a_up, rw_g_up,
           rw_k_k, rw_k_a, rw_r_k, rw_ln_g, rw_ln_b, m_conv_w, m_conv_b, m_dt_bias, m_A_log, m_D,
           m_norm, w_br, w_bm, w_out):
    bsz, l, d = x.shape
    lc = ctx.shape[1]
    i = 0
    tm = 512
    tpb = l // tm

    cc = jnp.concatenate([c, c_ctx[None], jnp.zeros((SUBLANES - bsz - 1, d), F32)], axis=0)
    mod = _mm(jax.nn.silu(cc), ada_w[i], SUBLANES, 1024) + ada_b[i]
    mod = mod.reshape(SUBLANES, N_MOD, d)
    lat_row = lambda t: t // tpb
    ctx_row = lambda t: bsz

    bf = lambda w: w.astype(BF16)
    h = x.reshape(bsz * l, d)
    hc = ctx.reshape(bsz * lc, d)
    wg1, wu1, wd1 = bf(ffn1_gate[i]), bf(ffn1_up[i]), bf(ffn1_down[i])
    h = _ffn(h, mod, lat_row, 0, norm_pre[i, 0], norm_post[i, 0], wg1, wu1, wd1, tm, 512)
    hc = _ffn(hc, mod, ctx_row, 0, norm_pre[i, 0], norm_post[i, 0], wg1, wu1, wd1, tm, 512)

    wi = w_in[i].astype(BF16)
    o_m = RW_COLS
    o_g = RW_COLS + M_COLS
    w_cat = jnp.concatenate([wi[:, o_g:], wi[:, :RW_COLS],
                             jnp.zeros((d, LORA_W - (RW_COLS - 3 * RW_DIM)), BF16),
                             wi[:, o_m:o_m + M_DIM + M_XBC], wi[:, o_m + M_DIM + M_XBC:o_g]],
                            axis=1)
    tn = 1920
    u2d = _norm_mm(h, mod, lat_row, 1, norm_pre[i, 1], w_cat, tm, tn)
    uc2d = _norm_mm(hc, mod, ctx_row, 1, norm_pre[i, 1], w_cat, tm, tn)
    u = u2d.reshape(bsz, l, U_COLS)
    uc = uc2d.reshape(bsz, lc, U_COLS)

    cw = FEAT_CW
    mu = jnp.pad(rw_mu[i], (0, 3 * RW_DIM + LORA_W - RW_COLS)).reshape(1, -1)
    zero = lambda rows: jnp.zeros((rows, RW_DIM), F32)
    wl_rows = lambda w, before, after: jnp.concatenate([zero(before), w, zero(after)], axis=0)
    w_lora = jnp.stack([wl_rows(rw_w_up[i, 0], 0, LORA_W - W_LORA),
                        wl_rows(rw_w_up[i, 1], 0, LORA_W - W_LORA),
                        wl_rows(rw_a_up[i], W_LORA, LORA_W - W_LORA - A_LORA),
                        wl_rows(rw_g_up[i], W_LORA + A_LORA, LORA_W - W_LORA - A_LORA - G_LORA)],
                       axis=1)
    w_lora = (w_lora.reshape(LORA_W, 4, RW_DIM // cw, cw).transpose(2, 0, 1, 3)
              .reshape(RW_DIM // cw, LORA_W, 4 * cw).astype(BF16))
    pv = jnp.stack([rw_w0[i, 0], rw_w0[i, 1], rw_a0[i], rw_k_k[i], rw_k_a[i],
                    rw_r_k[i].reshape(RW_DIM), jnp.zeros((RW_DIM,), F32),
                    jnp.zeros((RW_DIM,), F32)], axis=0)
    seq3 = lambda t, n: t.reshape(bsz, n, RW_DIM)
    r, k, v, kk, bvec, lwf, lwb, g, bonus = [
        seq3(t, l) for t in _rwkv_feat(u2d, "grid", l, tm, mu, w_lora, pv)]
    rc, kc, vc, kkc, bc, lwfc, lwbc, _, _ = [
        seq3(t, lc) for t in _rwkv_feat(uc2d, "seq", lc, lc, mu, w_lora, pv)]
    s0r = jnp.zeros((bsz, RW_DIM // LANES, LANES, LANES), F32)
    _, src_f = _rwkv_scan(rc, kc, vc, kkc, bc, lwfc, s0r, lc, False, emit=False)
    _, src_b = _rwkv_scan(rc, kc, vc, kkc, bc, lwbc, s0r, lc, True, emit=False)
    yr, _ = _rwkv_scan(r, k, v, kk, bvec, lwf, src_f, tm, False)
    o_r, _ = _rwkv_scan(r, k, v, kk, bvec, lwb, src_b, tm, True,
                        finish=(yr, bonus, g, rw_ln_g[i], rw_ln_b[i]))
    br = _mm(o_r.reshape(bsz * l, RW_DIM), bf(w_br[i]), tm, D_MODEL)

    wconv = jnp.concatenate([m_conv_w[i], m_conv_b[i][None],
                             jnp.zeros((SUBLANES - 4, M_XBC), F32)], axis=0)
    xbc = _conv(u2d, l, 2 * tm, wconv).reshape(bsz, l, M_XBC)
    xbcc = _conv(uc2d, lc, lc, wconv).reshape(bsz, lc, M_XBC)
    dtf, dtb = _dt_features(u, m_dt_bias[i])
    dtfc, dtbc = _dt_features(uc, m_dt_bias[i])
    a_f = -jnp.exp(m_A_log[i, 0])
    a_b = -jnp.exp(m_A_log[i, 1])
    s0m = jnp.zeros((bsz, M_GROUPS, M_STATE, M_HPG * M_HEADDIM), F32)
    _, smc_f = _ssd_scan(xbcc, dtfc, a_f, s0m, lc, False, emit=False)
    _, smc_b = _ssd_scan(xbcc, dtbc, a_b, s0m, lc, True, emit=False)
    ym, _ = _ssd_scan(xbc, dtf, a_f, smc_f, tm, False)
    dskip = jnp.repeat(m_D[i], M_HEADDIM).reshape(1, M_DIM)
    ym, _ = _ssd_scan(xbc, dtb, a_b, smc_b, tm, True, finish=(ym, u, dskip))
    bmo = _norm_mm(ym.reshape(bsz * l, M_DIM), None, None, 0, m_norm[i], bf(w_bm[i]), tm, 512)

    h = _merge(br, bmo, u2d, h, mod, lambda t: t // (l // 256), 1, norm_post[i, 1], bf(w_out[i]), 256)
    h = _ffn(h, mod, lat_row, 2, norm_pre[i, 2], norm_post[i, 2], bf(ffn2_gate[i]),
             bf(ffn2_up[i]), bf(ffn2_down[i]), tm, 512)
    return h.reshape(bsz, l, d)
```

```python
import functools
import math

import jax
import jax.numpy as jnp
from jax import lax
from jax.experimental import pallas as pl
from jax.experimental.pallas import tpu as pltpu

F32 = jnp.float32
BF16 = jnp.bfloat16

D_MODEL = 2048
GRID_W = 64
N_MOD = 9
MACARON_W = 0.5
D_FF = 5632
RMS_EPS = 1e-6
GN_EPS = 64e-5
L2_EPS = 1e-12

RW_HEAD = 64
RW_HEADS = D_MODEL // RW_HEAD
RW_DIM = RW_HEADS * RW_HEAD
W_LORA = 96
A_LORA = 96
G_LORA = 256
RW_COLS = 3 * RW_DIM + W_LORA + A_LORA + G_LORA

M_DIM = 2 * D_MODEL
M_HEADDIM = 64
M_HEADS = M_DIM // M_HEADDIM
M_GROUPS = 8
M_HPG = M_HEADS // M_GROUPS
M_STATE = 128
M_BC = M_GROUPS * M_STATE
M_XBC = M_DIM + 2 * M_BC
M_COLS = M_DIM + M_XBC + 2 * M_HEADS
GATE_COLS = 2 * D_MODEL

CHUNK = 64
LANES = 128
SUBLANES = 8
VMEM_LIMIT = 48 * 1024 * 1024

LORA_W = 512
RW_BASE = GATE_COLS
LORA_BASE = RW_BASE + 3 * RW_DIM
Z_BASE = LORA_BASE + LORA_W
XBC_BASE = Z_BASE + M_DIM
DT_BASE = XBC_BASE + M_XBC
U_COLS = DT_BASE + 2 * M_HEADS
FEAT_CW = 256
CONV_CW = 512


def _cparams(sem):
    return pltpu.CompilerParams(dimension_semantics=sem, vmem_limit_bytes=VMEM_LIMIT)


def _dot(a, b):
    return jnp.dot(a.astype(BF16), b.astype(BF16), preferred_element_type=F32)


def _dot_nt(a, b):
    return lax.dot_general(a.astype(BF16), b.astype(BF16), (((1,), (1,)), ((), ())),
                           preferred_element_type=F32)


def _dot_tn(a, b):
    return lax.dot_general(a.astype(BF16), b.astype(BF16), (((0,), (0,)), ((), ())),
                           preferred_element_type=F32)


def _split3(x):
    hi = x.astype(BF16)
    r1 = x - hi.astype(F32)
    mid = r1.astype(BF16)
    lo = (r1 - mid.astype(F32)).astype(BF16)
    return hi, mid, lo


def _dot_01_lhs(t3, x):
    return jnp.dot(t3, jnp.concatenate(_split3(x), axis=0), preferred_element_type=F32)


def _tri3(reverse):
    r = lax.broadcasted_iota(jnp.int32, (CHUNK, 3 * CHUNK), 0)
    c = lax.broadcasted_iota(jnp.int32, (CHUNK, 3 * CHUNK), 1) % CHUNK
    return ((r <= c) if reverse else (r >= c)).astype(BF16)


def _dot_01_rhs(x, e):
    hi, mid, lo = _split3(x)
    return (jnp.dot(hi, e, preferred_element_type=F32) + jnp.dot(mid, e, preferred_element_type=F32)
            + jnp.dot(lo, e, preferred_element_type=F32))


def _group_ones(width, group):
    r = lax.broadcasted_iota(jnp.int32, (width, width), 0)
    c = lax.broadcasted_iota(jnp.int32, (width, width), 1)
    return ((r // group) == (c // group)).astype(BF16)


def _group_sum(x, ones):
    hi = x.astype(BF16)
    lo = (x - hi.astype(F32)).astype(BF16)
    return (jnp.dot(hi, ones, preferred_element_type=F32)
            + jnp.dot(lo, ones, preferred_element_type=F32))


def _sigmoid(x):
    return 0.5 * jnp.tanh(0.5 * x) + 0.5


def _softplus(x):
    return jnp.maximum(x, 0.0) + jnp.log(1.0 + jnp.exp(-jnp.abs(x)))


def _mm_kernel(a_ref, b_ref, o_ref):
    o_ref[...] = _dot(a_ref[...], b_ref[...]).astype(o_ref.dtype)


def _mm(a, b, tm, tn, out_dtype=F32):
    m, k = a.shape
    _, n = b.shape
    assert m % tm == 0 and n % tn == 0, (m, n, tm, tn)
    return pl.pallas_call(
        _mm_kernel,
        grid=(m // tm, n // tn),
        in_specs=[pl.BlockSpec((tm, k), lambda i, j: (i, 0)),
                  pl.BlockSpec((k, tn), lambda i, j: (0, j))],
        out_specs=pl.BlockSpec((tm, tn), lambda i, j: (i, j)),
        out_shape=jax.ShapeDtypeStruct((m, n), out_dtype),
        compiler_params=_cparams(("parallel", "arbitrary")),
        name="mm",
    )(a, b)


def _regroup_kernel(x_ref, o_ref):
    o_g = RW_COLS + M_COLS
    o_dt = RW_COLS + M_DIM + M_XBC
    o_ref[:, :GATE_COLS] = x_ref[:, o_g:].astype(BF16)
    o_ref[:, RW_BASE:RW_BASE + RW_COLS] = x_ref[:, :RW_COLS].astype(BF16)
    o_ref[:, RW_BASE + RW_COLS:Z_BASE] = jnp.zeros((x_ref.shape[0], Z_BASE - RW_BASE - RW_COLS), BF16)
    o_ref[:, Z_BASE:DT_BASE] = x_ref[:, RW_COLS:o_dt].astype(BF16)
    o_ref[:, DT_BASE:] = x_ref[:, o_dt:o_g].astype(BF16)


def _regroup(w, tr=64):
    d, n = w.shape
    assert n == RW_COLS + M_COLS + GATE_COLS and d % tr == 0
    return pl.pallas_call(
        _regroup_kernel,
        grid=(d // tr,),
        in_specs=[pl.BlockSpec((tr, n), lambda i: (i, 0))],
        out_specs=pl.BlockSpec((tr, U_COLS), lambda i: (i, 0)),
        out_shape=jax.ShapeDtypeStruct((d, U_COLS), BF16),
        compiler_params=_cparams(("parallel",)),
        name="regroup",
    )(w)


def _rms(t):
    return t * lax.rsqrt(jnp.mean(t * t, axis=-1, keepdims=True) + RMS_EPS)


def _ffn_kernel(h_ref, mod_ref, gpre_ref, gpost_ref, wg_ref, wu_ref, wd_ref, o_ref,
                xn_ref, acc_ref, *, sub, weight):
    j = pl.program_id(1)

    @pl.when(j == 0)
    def _():
        y = _rms(h_ref[...]) * gpre_ref[...]
        xn = y * (1.0 + mod_ref[0, 3 * sub + 1:3 * sub + 2, :]) + mod_ref[0, 3 * sub:3 * sub + 1, :]
        xn_ref[...] = xn.astype(BF16)
        acc_ref[...] = jnp.zeros_like(acc_ref)

    xn = xn_ref[...]
    g = jnp.dot(xn, wg_ref[...], preferred_element_type=F32)
    u = jnp.dot(xn, wu_ref[...], preferred_element_type=F32)
    a = (g * jax.nn.sigmoid(g)) * u
    acc_ref[...] += jnp.dot(a.astype(BF16), wd_ref[...], preferred_element_type=F32)

    @pl.when(j == pl.num_programs(1) - 1)
    def _():
        yn = _rms(acc_ref[...]) * gpost_ref[...]
        o_ref[...] = h_ref[...] + (weight * mod_ref[0, 3 * sub + 2:3 * sub + 3, :]) * yn


def _ffn(h, mod, mod_row, sub, g_pre, g_post, wg, wu, wd, tm, tf):
    m, d = h.shape
    f = wg.shape[1]
    assert m % tm == 0 and f % tf == 0
    return pl.pallas_call(
        functools.partial(_ffn_kernel, sub=sub, weight=MACARON_W),
        grid=(m // tm, f // tf),
        in_specs=[pl.BlockSpec((tm, d), lambda i, j: (i, 0)),
                  pl.BlockSpec((1, N_MOD, d), lambda i, j: (mod_row(i), 0, 0)),
                  pl.BlockSpec((1, d), lambda i, j: (0, 0)),
                  pl.BlockSpec((1, d), lambda i, j: (0, 0)),
                  pl.BlockSpec((d, tf), lambda i, j: (0, j)),
                  pl.BlockSpec((d, tf), lambda i, j: (0, j)),
                  pl.BlockSpec((tf, d), lambda i, j: (j, 0))],
        out_specs=pl.BlockSpec((tm, d), lambda i, j: (i, 0)),
        out_shape=jax.ShapeDtypeStruct((m, d), F32),
        scratch_shapes=[pltpu.VMEM((tm, d), BF16), pltpu.VMEM((tm, d), F32)],
        compiler_params=_cparams(("parallel", "arbitrary")),
        name="ffn",
    )(h, mod, g_pre.reshape(1, d), g_post.reshape(1, d), wg, wu, wd)


def _norm_mm_kernel(*refs, sub, modulated):
    if modulated:
        h_ref, mod_ref, gpre_ref, w_ref, o_ref, xn_ref = refs
    else:
        h_ref, gpre_ref, w_ref, o_ref, xn_ref = refs

    @pl.when(pl.program_id(1) == 0)
    def _():
        xn = _rms(h_ref[...]) * gpre_ref[...]
        if modulated:
            xn = xn * (1.0 + mod_ref[0, 3 * sub + 1:3 * sub + 2, :]) + mod_ref[0, 3 * sub:3 * sub + 1, :]
        xn_ref[...] = xn.astype(BF16)

    o_ref[...] = jnp.dot(xn_ref[...], w_ref[...], preferred_element_type=F32)


def _norm_mm(h, mod, mod_row, sub, g_pre, w, tm, tn):
    m, d = h.shape
    n = w.shape[1]
    assert m % tm == 0 and n % tn == 0
    modulated = mod is not None
    mod_specs = [pl.BlockSpec((1, N_MOD, d), lambda i, j: (mod_row(i), 0, 0))] if modulated else []
    return pl.pallas_call(
        functools.partial(_norm_mm_kernel, sub=sub, modulated=modulated),
        grid=(m // tm, n // tn),
        in_specs=[pl.BlockSpec((tm, d), lambda i, j: (i, 0))] + mod_specs
        + [pl.BlockSpec((1, d), lambda i, j: (0, 0)),
           pl.BlockSpec((d, tn), lambda i, j: (0, j))],
        out_specs=pl.BlockSpec((tm, tn), lambda i, j: (i, j)),
        out_shape=jax.ShapeDtypeStruct((m, n), F32),
        scratch_shapes=[pltpu.VMEM((tm, d), BF16)],
        compiler_params=_cparams(("parallel", "arbitrary")),
        name="norm_mm",
    )(*([h] + ([mod] if modulated else []) + [g_pre.reshape(1, d), w]))


def _merge_kernel(br_ref, bm_ref, gr_ref, gm_ref, h_ref, mod_ref, gpost_ref, w_ref, o_ref, *, sub):
    merged = _sigmoid(gr_ref[...]) * br_ref[...] + _sigmoid(gm_ref[...]) * bm_ref[...]
    y = jnp.dot(merged.astype(BF16), w_ref[...], preferred_element_type=F32)
    yn = _rms(y) * gpost_ref[...]
    o_ref[...] = h_ref[...] + mod_ref[0, 3 * sub + 2:3 * sub + 3, :] * yn


def _merge(br, bm, u2d, h, mod, mod_row, sub, g_post, w, tm):
    m, d = h.shape
    tile = pl.BlockSpec((tm, d), lambda i: (i, 0))
    return pl.pallas_call(
        functools.partial(_merge_kernel, sub=sub),
        grid=(m // tm,),
        in_specs=[tile, tile,
                  pl.BlockSpec((tm, d), lambda i: (i, 0)),
                  pl.BlockSpec((tm, d), lambda i: (i, 1)),
                  tile,
                  pl.BlockSpec((1, N_MOD, d), lambda i: (mod_row(i), 0, 0)),
                  pl.BlockSpec((1, d), lambda i: (0, 0)),
                  pl.BlockSpec((d, d), lambda i: (0, 0))],
        out_specs=tile,
        out_shape=jax.ShapeDtypeStruct((m, d), F32),
        compiler_params=_cparams(("parallel",)),
        name="merge",
    )(br, bm, u2d, u2d, h, mod, g_post.reshape(1, d), w)


def _rwkv_feat_kernel(*refs, mode, period, tiles_per_seq, tt):
    if mode == "grid":
        (r_ref, k_ref, v_ref, lo_ref, kprev_ref, vprev_ref, vnext_ref, lonext_ref,
         mur_ref, muk_ref, muv_ref, mulo_ref, wl_ref, pv_ref) = refs[:14]
        rest = refs[14:]
    else:
        (r_ref, k_ref, v_ref, lo_ref, mur_ref, muk_ref, muv_ref, mulo_ref, wl_ref, pv_ref) = refs[:10]
        rest = refs[10:]
    r_o, k_o, v_o, kk_o, b_o, lwf_o, lwb_o, g_o, bonus_o, act_scr = rest
    i = pl.program_id(0)
    j = pl.program_id(1)
    cw = r_ref.shape[1]
    t = lax.broadcasted_iota(jnp.int32, (tt, 1), 0)
    row_first = (t % period) == 0
    row_last = (t % period) == period - 1
    seq_first = (i % tiles_per_seq) == 0
    seq_last = (i % tiles_per_seq) == tiles_per_seq - 1
    quarter = RW_COLS // 4

    def prev1(x):
        return jnp.where(row_first, 0.0, pltpu.roll(x, 1, 0))

    def next1(x):
        return jnp.where(row_last, 0.0, pltpu.roll(x, tt - 1, 0))

    def up(x, halo_ref):
        halo = jnp.where(seq_first, 0.0, halo_ref[...])
        return jnp.concatenate([halo, x[:tt - GRID_W]], axis=0)

    def down(x, halo_ref):
        halo = jnp.where(seq_last, 0.0, halo_ref[...])
        return jnp.concatenate([x[GRID_W:], halo], axis=0)

    def col_index(base, width):
        return base + lax.broadcasted_iota(jnp.int32, (1, width), 1)

    def lerp(x, xs, mu_ref):
        return x + (xs - x) * mu_ref[...]

    @pl.when(j == 0)
    def _():
        x = lo_ref[...]
        xs = down(x, lonext_ref) if mode == "grid" else next1(x)
        x = lerp(x, xs, mulo_ref)
        c = col_index(0, LORA_W)
        act = jnp.where(c < W_LORA, jnp.tanh(x), jnp.where(c < W_LORA + A_LORA, x, _sigmoid(x)))
        act_scr[...] = act.astype(BF16)

    x = r_ref[...]
    c = col_index(j * cw, cw)
    if mode == "grid":
        xs = jnp.where(c < quarter, prev1(x), next1(x))
    else:
        xs = prev1(x)
    r = lerp(x, xs, mur_ref)

    x = k_ref[...]
    c = col_index(RW_DIM + j * cw, cw)
    if mode == "grid":
        xs = jnp.where(c < 2 * quarter, next1(x), up(x, kprev_ref))
    else:
        xs = jnp.where(c < 2 * quarter, prev1(x), next1(x))
    k = lerp(x, xs, muk_ref)

    x = v_ref[...]
    c = col_index(2 * RW_DIM + j * cw, cw)
    if mode == "grid":
        xs = jnp.where(c < 3 * quarter, up(x, vprev_ref), down(x, vnext_ref))
    else:
        xs = next1(x)
    v = lerp(x, xs, muv_ref)

    lo = jnp.dot(act_scr[...], wl_ref[...], preferred_element_type=F32)
    w0f, w0b, a0 = pv_ref[0:1, :], pv_ref[1:2, :], pv_ref[2:3, :]
    k_k, k_a, r_k = pv_ref[3:4, :], pv_ref[4:5, :], pv_ref[5:6, :]
    a = _sigmoid(a0 + lo[:, 2 * cw:3 * cw])
    ones = _group_ones(cw, RW_HEAD)
    kkr = k * k_k
    kk = kkr * lax.rsqrt(jnp.maximum(_group_sum(kkr * kkr, ones), L2_EPS * L2_EPS))
    k = k * (1.0 + (a - 1.0) * k_a)
    lwf_o[...] = -math.exp(-0.5) * _sigmoid(w0f + lo[:, 0:cw])
    lwb_o[...] = -math.exp(-0.5) * _sigmoid(w0b + lo[:, cw:2 * cw])
    r_o[...] = r.astype(r_o.dtype)
    k_o[...] = k.astype(k_o.dtype)
    v_o[...] = v.astype(v_o.dtype)
    kk_o[...] = kk.astype(kk_o.dtype)
    b_o[...] = (kk * a).astype(b_o.dtype)
    g_o[...] = lo[:, 3 * cw:4 * cw]
    bonus_o[...] = _group_sum(r * k * r_k, ones) * v


def _rwkv_feat(u2d, mode, seq_len, tt, mu, w_lora, pv):
    m = u2d.shape[0]
    cw = FEAT_CW
    ncb = RW_DIM // cw
    assert m % tt == 0 and seq_len % tt == 0
    tiles_per_seq = seq_len // tt
    rb, kb, vb = RW_BASE // cw, (RW_BASE + RW_DIM) // cw, (RW_BASE + 2 * RW_DIM) // cw
    lob = LORA_BASE // LORA_W
    hr = tt // GRID_W
    nhb = m // GRID_W
    main = [pl.BlockSpec((tt, cw), lambda i, j: (i, rb + j)),
            pl.BlockSpec((tt, cw), lambda i, j: (i, kb + j)),
            pl.BlockSpec((tt, cw), lambda i, j: (i, vb + j)),
            pl.BlockSpec((tt, LORA_W), lambda i, j: (i, lob))]
    args = [u2d, u2d, u2d, u2d]
    if mode == "grid":
        assert tt % GRID_W == 0
        period = GRID_W
        prev_row = lambda i: jnp.maximum(i * hr - 1, 0)
        next_row = lambda i: jnp.minimum((i + 1) * hr, nhb - 1)
        main += [pl.BlockSpec((GRID_W, cw), lambda i, j: (prev_row(i), kb + j)),
                 pl.BlockSpec((GRID_W, cw), lambda i, j: (prev_row(i), vb + j)),
                 pl.BlockSpec((GRID_W, cw), lambda i, j: (next_row(i), vb + j)),
                 pl.BlockSpec((GRID_W, LORA_W), lambda i, j: (next_row(i), lob))]
        args += [u2d, u2d, u2d, u2d]
    else:
        assert tiles_per_seq == 1
        period = tt
    params = [pl.BlockSpec((1, cw), lambda i, j: (0, j)),
              pl.BlockSpec((1, cw), lambda i, j: (0, RW_DIM // cw + j)),
              pl.BlockSpec((1, cw), lambda i, j: (0, 2 * RW_DIM // cw + j)),
              pl.BlockSpec((1, LORA_W), lambda i, j: (0, 3 * RW_DIM // LORA_W)),
              pl.BlockSpec((None, LORA_W, 4 * cw), lambda i, j: (j, 0, 0)),
              pl.BlockSpec((SUBLANES, cw), lambda i, j: (0, j))]
    args += [mu, mu, mu, mu, w_lora, pv]
    out_spec = pl.BlockSpec((tt, cw), lambda i, j: (i, j))
    return pl.pallas_call(
        functools.partial(_rwkv_feat_kernel, mode=mode, period=period,
                          tiles_per_seq=tiles_per_seq, tt=tt),
        grid=(m // tt, ncb),
        in_specs=main + params,
        out_specs=[out_spec] * 9,
        out_shape=[jax.ShapeDtypeStruct((m, RW_DIM), BF16)] * 5
        + [jax.ShapeDtypeStruct((m, RW_DIM), F32)] * 4,
        scratch_shapes=[pltpu.VMEM((tt, LORA_W), BF16)],
        compiler_params=_cparams(("parallel", "arbitrary")),
        name="rwkv_feat",
    )(*args)


def _conv_kernel(x_ref, prev_ref, next_ref, w_ref, o_ref, *, tiles_per_seq, tt):
    i = pl.program_id(0)
    x = x_ref[...]
    t = lax.broadcasted_iota(jnp.int32, (tt, 1), 0)
    seq_first = (i % tiles_per_seq) == 0
    seq_last = (i % tiles_per_seq) == tiles_per_seq - 1
    hp = jnp.where(seq_first, 0.0, prev_ref[SUBLANES - 1:SUBLANES, :])
    hn = jnp.where(seq_last, 0.0, next_ref[0:1, :])
    prev = jnp.where(t == 0, hp, pltpu.roll(x, 1, 0))
    nxt = jnp.where(t == tt - 1, hn, pltpu.roll(x, tt - 1, 0))
    y = prev * w_ref[0:1, :] + x * w_ref[1:2, :] + nxt * w_ref[2:3, :] + w_ref[3:4, :]
    o_ref[...] = y * _sigmoid(y)


def _conv(u2d, seq_len, tt, wconv):
    m = u2d.shape[0]
    cw = CONV_CW
    base = XBC_BASE // cw
    tiles_per_seq = seq_len // tt
    hr = tt // SUBLANES
    nhb = m // SUBLANES
    return pl.pallas_call(
        functools.partial(_conv_kernel, tiles_per_seq=tiles_per_seq, tt=tt),
        grid=(m // tt, M_XBC // cw),
        in_specs=[pl.BlockSpec((tt, cw), lambda i, j: (i, base + j)),
                  pl.BlockSpec((SUBLANES, cw), lambda i, j: (jnp.maximum(i * hr - 1, 0), base + j)),
                  pl.BlockSpec((SUBLANES, cw),
                               lambda i, j: (jnp.minimum((i + 1) * hr, nhb - 1), base + j)),
                  pl.BlockSpec((SUBLANES, cw), lambda i, j: (0, j))],
        out_specs=pl.BlockSpec((tt, cw), lambda i, j: (i, j)),
        out_shape=jax.ShapeDtypeStruct((m, M_XBC), F32),
        compiler_params=_cparams(("parallel", "parallel")),
        name="conv",
    )(u2d, u2d, u2d, wconv)


def _stack2(x, m0):
    return jnp.concatenate([jnp.where(m0, x, 0.0), jnp.where(m0, 0.0, x)], axis=0)


def _rwkv_kernel(*refs, nchunks, pp, reverse, emit, finish):
    r_ref, k_ref, v_ref, kk_ref, b_ref, lw_ref, s0_ref = refs[:7]
    rest = list(refs[7:])
    if finish:
        yacc_ref, bonus_ref, g_ref, lng_ref, lnb_ref = rest[:5]
        rest = rest[5:]
    y_ref = rest.pop(0) if emit else None
    sfin_ref, s_scr = rest
    i = pl.program_id(2)

    @pl.when(i == 0)
    def _():
        s_scr[...] = s0_ref[...]

    lane = lax.broadcasted_iota(jnp.int32, (CHUNK, LANES), 1)
    trow = lax.broadcasted_iota(jnp.int32, (CHUNK, LANES), 0)
    m0 = lane < RW_HEAD
    scol = lane % RW_HEAD
    strict = (scol > trow) if reverse else (scol < trow)
    incl = (scol >= trow) if reverse else (scol <= trow)
    eye = (scol == trow).astype(F32)
    row2 = lax.broadcasted_iota(jnp.int32, (LANES, LANES), 0)
    col2 = lax.broadcasted_iota(jnp.int32, (LANES, LANES), 1)
    same = (row2 // RW_HEAD) == (col2 // RW_HEAD)
    tri3 = _tri3(reverse)
    last = 0 if reverse else CHUNK - 1
    bf = lambda x: x.astype(BF16)
    st2 = lambda x: _stack2(x, m0)

    sls = [(slice(c * CHUNK, (c + 1) * CHUNK), slice(hp * LANES, (hp + 1) * LANES))
           for c in range(nchunks) for hp in range(pp)]
    G = range(nchunks * pp)
    lw = [lw_ref[sl] for sl in sls]
    cum = [_dot_01_lhs(tri3, x) for x in lw]
    tot = [c[last:last + 1, :] for c in cum]
    eneg = [jnp.exp(-c) for c in cum]
    at = [bf(-kk_ref[sl] * jnp.exp(c - x)) for sl, c, x in zip(sls, cum, lw)]
    bk2 = [jnp.concatenate([st2(bf(b_ref[sl] * e)), st2(bf(k_ref[sl] * e))], axis=0)
           for sl, e in zip(sls, eneg)]
    if emit:
        rt = [r_ref[sl] * jnp.exp(c) for sl, c in zip(sls, cum)]
        big = [_dot_nt(jnp.concatenate([x, bf(y)], axis=0), z) for x, y, z in zip(at, rt, bk2)]
        rb = [bf(jnp.where(incl, x[CHUNK:, :LANES], 0.0)) for x in big]
        rk = [bf(jnp.where(incl, x[CHUNK:, LANES:], 0.0)) for x in big]
    else:
        big = [_dot_nt(x, z) for x, z in zip(at, bk2)]
    n = [jnp.where(strict, x[:CHUNK, :LANES], 0.0) for x in big]
    ak = [bf(jnp.where(strict, x[:CHUNK, LANES:], 0.0)) for x in big]
    v = [bf(v_ref[sl]) for sl in sls]
    v2 = [st2(x) for x in v]
    akv = [bf(_dot(x, y)) for x, y in zip(ak, v2)]
    p = [eye + x for x in n]
    npow = [bf(x) for x in n]
    npow = [bf(_dot(x, st2(x))) for x in npow]
    for it in range(4):
        both = [_dot(jnp.concatenate([x, bf(y)], axis=0), st2(x)) for x, y in zip(npow, p)]
        p = [y + x[CHUNK:] for x, y in zip(both, p)]
        npow = [bf(x[:CHUNK]) for x in both]
    p = [bf(y + _dot(bf(y), st2(x))) for x, y in zip(npow, p)]
    wu = [_dot(x, jnp.concatenate([st2(y), st2(z)], axis=1)) for x, y, z in zip(p, at, akv)]
    w = [bf(x[:, :LANES]) for x in wu]
    up = [bf(x[:, LANES:]) for x in wu]
    eend = [jnp.exp(t - c) for t, c in zip(tot, cum)]
    bg = [bf(b_ref[sl] * e) for sl, e in zip(sls, eend)]
    kg = [bf(k_ref[sl] * e) for sl, e in zip(sls, eend)]
    mc = [bf(jnp.where(same, _dot_tn(x, y), 0.0)) for x, y in zip(w, bg)]
    nc = [jnp.where(same, _dot_tn(jnp.concatenate([x, y], axis=0),
                                  jnp.concatenate([z, t], axis=0)), 0.0)
          for x, y, z, t in zip(up, v, bg, kg)]
    gam = [jnp.exp(t) for t in tot]
    if emit:
        q = [bf(x + _dot(y, st2(z))) for x, y, z in zip(rt, rb, w)]
        yp = [_dot(jnp.concatenate([x, y], axis=1), jnp.concatenate([st2(z), t], axis=0))
              for x, y, z, t in zip(rb, rk, up, v2)]

    s = [s_scr[hp] for hp in range(pp)]
    ys = [None] * len(G)
    for c in (reversed(range(nchunks)) if reverse else range(nchunks)):
        for hp in range(pp):
            g = c * pp + hp
            sb = s[hp].astype(BF16)
            if emit:
                ys[g] = _dot_nt(q[g], sb) + yp[g]
            s[hp] = s[hp] * gam[g] + _dot(sb, mc[g]) + nc[g]
    for hp in range(pp):
        s_scr[hp] = s[hp]

    if emit and not finish:
        for g in G:
            y_ref[sls[g]] = ys[g]
    if finish:
        ones = _group_ones(LANES, RW_HEAD)
        ysum = [y + yacc_ref[sl] for y, sl in zip(ys, sls)]
        mean = [_group_sum(y, ones) * (1.0 / RW_HEAD) for y in ysum]
        dev = [y - mu for y, mu in zip(ysum, mean)]
        var = [_group_sum(d * d, ones) * (1.0 / RW_HEAD) for d in dev]
        for g in G:
            yn = dev[g] * lax.rsqrt(var[g] + GN_EPS)
            lanes = sls[g][1]
            o = (yn * lng_ref[:, lanes] + lnb_ref[:, lanes] + bonus_ref[sls[g]]) * g_ref[sls[g]]
            y_ref[sls[g]] = o.astype(y_ref.dtype)

    @pl.when(i == pl.num_programs(2) - 1)
    def _():
        for hp in range(pp):
            sfin_ref[hp] = s[hp]


RW_PAIRS_PER_STEP = 4


def _rwkv_scan(r, k, v, kk, b, lw, s0, tb, reverse, emit=True, finish=None):
    bsz, l, _ = r.shape
    pp = RW_PAIRS_PER_STEP
    npair = RW_DIM // LANES
    nblk = l // tb
    assert l % tb == 0 and tb % CHUNK == 0 and npair % pp == 0
    blk = (lambda i: nblk - 1 - i) if reverse else (lambda i: i)
    seq = pl.BlockSpec((None, tb, pp * LANES), lambda bi, p, i: (bi, blk(i), p))
    st = pl.BlockSpec((None, pp, LANES, LANES), lambda bi, p, i: (bi, p, 0, 0))
    vec = pl.BlockSpec((1, pp * LANES), lambda bi, p, i: (0, p))
    st_shape = jax.ShapeDtypeStruct((bsz, npair, LANES, LANES), F32)
    y_shape = jax.ShapeDtypeStruct((bsz, l, RW_DIM), BF16 if finish else F32)
    args = [r, k, v, kk, b, lw, s0]
    in_specs = [seq] * 6 + [st]
    if finish:
        y_other, bonus, g, ln_g, ln_b = finish
        args += [y_other, bonus, g, ln_g.reshape(1, RW_DIM), ln_b.reshape(1, RW_DIM)]
        in_specs += [seq, seq, seq, vec, vec]
    outs = pl.pallas_call(
        functools.partial(_rwkv_kernel, nchunks=tb // CHUNK, pp=pp, reverse=reverse, emit=emit,
                          finish=bool(finish)),
        grid=(bsz, npair // pp, nblk),
        in_specs=in_specs,
        out_specs=([seq] if emit else []) + [st],
        out_shape=([y_shape] if emit else []) + [st_shape],
        scratch_shapes=[pltpu.VMEM((pp, LANES, LANES), F32)],
        compiler_params=_cparams(("parallel", "parallel", "arbitrary")),
        name="rwkv_scan",
    )(*args)
    return outs if emit else (None, outs[0])


def _ssd_kernel(*refs, nchunks, reverse, emit, finish):
    x_ref, b_ref, c_ref, udt_ref, bias_ref, aall_ref, s0_ref = refs[:7]
    rest = list(refs[7:])
    if finish:
        yacc_ref, z_ref, dskip_ref = rest[:3]
        rest = rest[3:]
    y_ref = rest.pop(0) if emit else None
    sfin_ref, s_scr = rest
    grp = pl.program_id(1)
    i = pl.program_id(2)

    @pl.when(i == 0)
    def _():
        s_scr[...] = s0_ref[...]

    width = M_HPG * M_HEADDIM
    m0 = lax.broadcasted_iota(jnp.int32, (CHUNK, LANES), 1) < M_HEADDIM
    tri3 = _tri3(reverse)
    qr = lax.broadcasted_iota(jnp.int32, (3 * CHUNK, CHUNK), 0) % CHUNK
    qc = lax.broadcasted_iota(jnp.int32, (3 * CHUNK, CHUNK), 1)
    tri3_t = ((qr >= qc) if reverse else (qr <= qc)).astype(BF16)
    head0 = (M_HEADS if reverse else 0) + grp * M_HPG
    hr = lax.broadcasted_iota(jnp.int32, (3 * LANES, M_HPG), 0) % LANES
    hc = lax.broadcasted_iota(jnp.int32, (3 * LANES, M_HPG), 1)
    sel3 = (hr == head0 + hc).astype(BF16)
    er = lax.broadcasted_iota(jnp.int32, (3 * M_HPG, width), 0) % M_HPG
    ec = lax.broadcasted_iota(jnp.int32, (3 * M_HPG, width), 1) // M_HEADDIM
    expand3 = (er == ec).astype(BF16)
    lt = lax.broadcasted_iota(jnp.int32, (CHUNK, LANES), 0)
    ll = lax.broadcasted_iota(jnp.int32, (CHUNK, LANES), 1)
    keep = ((ll % CHUNK) >= lt) if reverse else ((ll % CHUNK) <= lt)
    last = 0 if reverse else CHUNK - 1
    split_cat = lambda x, axis: jnp.concatenate(_split3(x), axis=axis)

    G = range(nchunks)
    sls = [slice(g * CHUNK, (g + 1) * CHUNK) for g in G]
    dt_all = [_softplus(udt_ref[sl] + bias_ref[...]) for sl in sls]
    both = [jnp.dot(split_cat(jnp.concatenate([x, x * aall_ref[...]], axis=0), 1), sel3,
                    preferred_element_type=F32) for x in dt_all]
    dt = [x[:CHUNK] for x in both]
    da = [x[CHUNK:] for x in both]
    cum = [_dot_01_lhs(tri3, x) for x in da]
    dt_x = [jnp.dot(split_cat(x, 1), expand3, preferred_element_type=F32) for x in dt]
    cum_x = [jnp.dot(split_cat(x, 1), expand3, preferred_element_type=F32) for x in cum]
    tot_x = [x[last:last + 1, :] for x in cum_x]
    xdt = [x_ref[sl] * y for sl, y in zip(sls, dt_x)]
    xend = [x * jnp.exp(t - c) for x, t, c in zip(xdt, tot_x, cum_x)]
    bm = [b_ref[sl].astype(BF16) for sl in sls]
    cs = [_dot_tn(x, y) for x, y in zip(bm, xend)]
    gam = [jnp.exp(t) for t in tot_x]
    if emit:
        cm = [c_ref[sl].astype(BF16) for sl in sls]
        cum_t = [lax.dot_general(split_cat(x, 0), tri3_t, (((0,), (0,)), ((), ())),
                                 preferred_element_type=F32) for x in da]
        cb2 = [_dot_nt(x, jnp.concatenate([y, y], axis=0)) for x, y in zip(cm, bm)]
        ydiag = []
        for g in G:
            ys = []
            for j in range(M_HPG // 2):
                colsel = cum_x[g][:, j * LANES:(j + 1) * LANES]
                rowsel = jnp.concatenate([cum_t[g][2 * j:2 * j + 1, :],
                                          cum_t[g][2 * j + 1:2 * j + 2, :]], axis=1)
                seg = jnp.where(keep, colsel - rowsel, -jnp.inf)
                gp = cb2[g] * jnp.exp(seg)
                xp = xdt[g][:, j * LANES:(j + 1) * LANES]
                ys.append(_dot(gp, _stack2(xp, m0)))
            ydiag.append(jnp.concatenate(ys, axis=1))
        ecum = [jnp.exp(x) for x in cum_x]

    s = s_scr[...]
    for g in (reversed(G) if reverse else G):
        if emit:
            y = _dot(cm[g], s) * ecum[g] + ydiag[g]
            if finish:
                zz = z_ref[sls[g]]
                y = (y + yacc_ref[sls[g]] + dskip_ref[...] * x_ref[sls[g]]) * (zz * _sigmoid(zz))
            y_ref[sls[g]] = y
        s = s * gam[g] + cs[g]
    s_scr[...] = s

    @pl.when(i == pl.num_programs(2) - 1)
    def _():
        sfin_ref[...] = s


def _ssd_scan(xbc, u, dt_bias, a_all, s0, tb, reverse, emit=True, finish=None):
    bsz, l, _ = xbc.shape
    ncb = tb // CHUNK
    nblk = l // tb
    width = M_HPG * M_HEADDIM
    blk = (lambda i: nblk - 1 - i) if reverse else (lambda i: i)
    st = pl.BlockSpec((None, None, M_STATE, width), lambda bi, g, i: (bi, g, 0, 0))
    xspec = pl.BlockSpec((None, tb, width), lambda bi, g, i: (bi, blk(i), g))
    bspec = pl.BlockSpec((None, tb, M_STATE), lambda bi, g, i: (bi, blk(i), M_DIM // M_STATE + g))
    cspec = pl.BlockSpec((None, tb, M_STATE),
                         lambda bi, g, i: (bi, blk(i), (M_DIM + M_BC) // M_STATE + g))
    vec = pl.BlockSpec((1, 2 * M_HEADS), lambda bi, g, i: (0, 0))
    args = [xbc, xbc, xbc, u, dt_bias, a_all, s0]
    in_specs = [xspec, bspec, cspec,
                pl.BlockSpec((None, tb, 2 * M_HEADS), lambda bi, g, i: (bi, blk(i), DT_BASE // (2 * M_HEADS))),
                vec, vec, st]
    aliases = {}
    if finish:
        y_other, dskip = finish
        args += [y_other, u, dskip]
        in_specs += [xspec,
                     pl.BlockSpec((None, tb, width), lambda bi, g, i: (bi, blk(i), Z_BASE // width + g)),
                     pl.BlockSpec((1, width), lambda bi, g, i: (0, g))]
        aliases = {7: 0}
    outs = pl.pallas_call(
        functools.partial(_ssd_kernel, nchunks=ncb, reverse=reverse, emit=emit,
                          finish=bool(finish)),
        grid=(bsz, M_GROUPS, nblk),
        in_specs=in_specs,
        out_specs=([xspec] if emit else []) + [st],
        out_shape=([jax.ShapeDtypeStruct((bsz, l, M_DIM), F32)] if emit else [])
        + [jax.ShapeDtypeStruct((bsz, M_GROUPS, M_STATE, width), F32)],
        scratch_shapes=[pltpu.VMEM((M_STATE, width), F32)],
        input_output_aliases=aliases,
        compiler_params=_cparams(("parallel", "parallel", "arbitrary")),
        name="ssd_scan",
    )(*args)
    return outs if emit else (None, outs[0])


def kernel(x, c, ctx, c_ctx, ada_w, ada_b, norm_pre, norm_post, ffn1_gate, ffn1_up, ffn1_down,
           ffn2_gate, ffn2_up, ffn2_down, w_in, rw_mu, rw_w0, rw_w_up, rw_a0, rw_a_up, rw_g_up,
           rw_k_k, rw_k_a, rw_r_k, rw_ln_g, rw_ln_b, m_conv_w, m_conv_b, m_dt_bias, m_A_log, m_D,
           m_norm, w_br, w_bm, w_out):
    bsz, l, d = x.shape
    lc = ctx.shape[1]
    i = 0
    tm = 512
    tpb = l // tm

    cc = jnp.concatenate([c, c_ctx[None], jnp.zeros((SUBLANES - bsz - 1, d), F32)], axis=0)
    mod = _mm(jax.nn.silu(cc), ada_w[i], SUBLANES, 1024) + ada_b[i]
    mod = mod.reshape(SUBLANES, N_MOD, d)
    lat_row = lambda t: t // tpb
    ctx_row = lambda t: bsz

    bf = lambda w: w.astype(BF16)
    h = x.reshape(bsz * l, d)
    hc = ctx.reshape(bsz * lc, d)
    wg1, wu1, wd1 = bf(ffn1_gate[i]), bf(ffn1_up[i]), bf(ffn1_down[i])
    h = _ffn(h, mod, lat_row, 0, norm_pre[i, 0], norm_post[i, 0], wg1, wu1, wd1, tm, 512)
    hc = _ffn(hc, mod, ctx_row, 0, norm_pre[i, 0], norm_post[i, 0], wg1, wu1, wd1, tm, 512)

    w_cat = _regroup(w_in[i])
    tn = 1920
    u2d = _norm_mm(h, mod, lat_row, 1, norm_pre[i, 1], w_cat, tm, tn)
    uc2d = _norm_mm(hc, mod, ctx_row, 1, norm_pre[i, 1], w_cat, tm, tn)
    u = u2d.reshape(bsz, l, U_COLS)
    uc = uc2d.reshape(bsz, lc, U_COLS)

    cw = FEAT_CW
    mu = jnp.pad(rw_mu[i], (0, 3 * RW_DIM + LORA_W - RW_COLS)).reshape(1, -1)
    zero = lambda rows: jnp.zeros((rows, RW_DIM), F32)
    wl_rows = lambda w, before, after: jnp.concatenate([zero(before), w, zero(after)], axis=0)
    w_lora = jnp.stack([wl_rows(rw_w_up[i, 0], 0, LORA_W - W_LORA),
                        wl_rows(rw_w_up[i, 1], 0, LORA_W - W_LORA),
                        wl_rows(rw_a_up[i], W_LORA, LORA_W - W_LORA - A_LORA),
                        wl_rows(rw_g_up[i], W_LORA + A_LORA, LORA_W - W_LORA - A_LORA - G_LORA)],
                       axis=1)
    w_lora = (w_lora.reshape(LORA_W, 4, RW_DIM // cw, cw).transpose(2, 0, 1, 3)
              .reshape(RW_DIM // cw, LORA_W, 4 * cw).astype(BF16))
    pv = jnp.stack([rw_w0[i, 0], rw_w0[i, 1], rw_a0[i], rw_k_k[i], rw_k_a[i],
                    rw_r_k[i].reshape(RW_DIM), jnp.zeros((RW_DIM,), F32),
                    jnp.zeros((RW_DIM,), F32)], axis=0)
    seq3 = lambda t, n: t.reshape(bsz, n, RW_DIM)
    r, k, v, kk, bvec, lwf, lwb, g, bonus = [
        seq3(t, l) for t in _rwkv_feat(u2d, "grid", l, tm, mu, w_lora, pv)]
    rc, kc, vc, kkc, bc, lwfc, lwbc, _, _ = [
        seq3(t, lc) for t in _rwkv_feat(uc2d, "seq", lc, lc, mu, w_lora, pv)]
    s0r = jnp.zeros((bsz, RW_DIM // LANES, LANES, LANES), F32)
    _, src_f = _rwkv_scan(rc, kc, vc, kkc, bc, lwfc, s0r, lc, False, emit=False)
    _, src_b = _rwkv_scan(rc, kc, vc, kkc, bc, lwbc, s0r, lc, True, emit=False)
    yr, _ = _rwkv_scan(r, k, v, kk, bvec, lwf, src_f, tm, False)
    o_r, _ = _rwkv_scan(r, k, v, kk, bvec, lwb, src_b, tm, True,
                        finish=(yr, bonus, g, rw_ln_g[i], rw_ln_b[i]))
    br = _mm(o_r.reshape(bsz * l, RW_DIM), bf(w_br[i]), tm, D_MODEL)

    wconv = jnp.concatenate([m_conv_w[i], m_conv_b[i][None],
                             jnp.zeros((SUBLANES - 4, M_XBC), F32)], axis=0)
    xbc = _conv(u2d, l, 2 * tm, wconv).reshape(bsz, l, M_XBC)
    xbcc = _conv(uc2d, lc, lc, wconv).reshape(bsz, lc, M_XBC)
    dt_bias = m_dt_bias[i].reshape(1, 2 * M_HEADS)
    a_all = -jnp.exp(m_A_log[i]).reshape(1, 2 * M_HEADS)
    s0m = jnp.zeros((bsz, M_GROUPS, M_STATE, M_HPG * M_HEADDIM), F32)
    _, smc_f = _ssd_scan(xbcc, uc, dt_bias, a_all, s0m, lc, False, emit=False)
    _, smc_b = _ssd_scan(xbcc, uc, dt_bias, a_all, s0m, lc, True, emit=False)
    ym, _ = _ssd_scan(xbc, u, dt_bias, a_all, smc_f, tm, False)
    dskip = jnp.repeat(m_D[i], M_HEADDIM).reshape(1, M_DIM)
    ym, _ = _ssd_scan(xbc, u, dt_bias, a_all, smc_b, tm, True, finish=(ym, dskip))
    bmo = _norm_mm(ym.reshape(bsz * l, M_DIM), None, None, 0, m_norm[i], bf(w_bm[i]), tm, 512)

    h = _merge(br, bmo, u2d, h, mod, lambda t: t // (l // 256), 1, norm_post[i, 1], bf(w_out[i]), 256)
    h = _ffn(h, mod, lat_row, 2, norm_pre[i, 2], norm_post[i, 2], bf(ffn2_gate[i]),
             bf(ffn2_up[i]), bf(ffn2_down[i]), tm, 512)
    return h.reshape(bsz, l, d)
```

```python
import functools
import math

import jax
import jax.numpy as jnp
from jax import lax
from jax.experimental import pallas as pl
from jax.experimental.pallas import tpu as pltpu

F32 = jnp.float32
BF16 = jnp.bfloat16

D_MODEL = 2048
GRID_W = 64
N_MOD = 9
MACARON_W = 0.5
D_FF = 5632
RMS_EPS = 1e-6
GN_EPS = 64e-5
L2_EPS = 1e-12

RW_HEAD = 64
RW_HEADS = D_MODEL // RW_HEAD
RW_DIM = RW_HEADS * RW_HEAD
W_LORA = 96
A_LORA = 96
G_LORA = 256
RW_COLS = 3 * RW_DIM + W_LORA + A_LORA + G_LORA

M_DIM = 2 * D_MODEL
M_HEADDIM = 64
M_HEADS = M_DIM // M_HEADDIM
M_GROUPS = 8
M_HPG = M_HEADS // M_GROUPS
M_STATE = 128
M_BC = M_GROUPS * M_STATE
M_XBC = M_DIM + 2 * M_BC
M_COLS = M_DIM + M_XBC + 2 * M_HEADS
GATE_COLS = 2 * D_MODEL

CHUNK = 64
LANES = 128
SUBLANES = 8
VMEM_LIMIT = 48 * 1024 * 1024

LORA_W = 512
RW_BASE = GATE_COLS
LORA_BASE = RW_BASE + 3 * RW_DIM
Z_BASE = LORA_BASE + LORA_W
XBC_BASE = Z_BASE + M_DIM
DT_BASE = XBC_BASE + M_XBC
U_COLS = DT_BASE + 2 * M_HEADS
FEAT_CW = 256
CONV_CW = 512


def _cparams(sem):
    return pltpu.CompilerParams(dimension_semantics=sem, vmem_limit_bytes=VMEM_LIMIT)


def _dot(a, b):
    return jnp.dot(a.astype(BF16), b.astype(BF16), preferred_element_type=F32)


def _dot_nt(a, b):
    return lax.dot_general(a.astype(BF16), b.astype(BF16), (((1,), (1,)), ((), ())),
                           preferred_element_type=F32)


def _dot_tn(a, b):
    return lax.dot_general(a.astype(BF16), b.astype(BF16), (((0,), (0,)), ((), ())),
                           preferred_element_type=F32)


def _split3(x):
    hi = x.astype(BF16)
    r1 = x - hi.astype(F32)
    mid = r1.astype(BF16)
    lo = (r1 - mid.astype(F32)).astype(BF16)
    return hi, mid, lo


def _dot_01_lhs(t3, x):
    return jnp.dot(t3, jnp.concatenate(_split3(x), axis=0), preferred_element_type=F32)


def _tri3(reverse):
    r = lax.broadcasted_iota(jnp.int32, (CHUNK, 3 * CHUNK), 0)
    c = lax.broadcasted_iota(jnp.int32, (CHUNK, 3 * CHUNK), 1) % CHUNK
    return ((r <= c) if reverse else (r >= c)).astype(BF16)


def _dot_01_rhs(x, e):
    hi, mid, lo = _split3(x)
    return (jnp.dot(hi, e, preferred_element_type=F32) + jnp.dot(mid, e, preferred_element_type=F32)
            + jnp.dot(lo, e, preferred_element_type=F32))


def _group_ones(width, group):
    r = lax.broadcasted_iota(jnp.int32, (width, width), 0)
    c = lax.broadcasted_iota(jnp.int32, (width, width), 1)
    return ((r // group) == (c // group)).astype(BF16)


def _group_sum(x, ones):
    hi = x.astype(BF16)
    lo = (x - hi.astype(F32)).astype(BF16)
    return (jnp.dot(hi, ones, preferred_element_type=F32)
            + jnp.dot(lo, ones, preferred_element_type=F32))


def _sigmoid(x):
    return 0.5 * jnp.tanh(0.5 * x) + 0.5


def _softplus(x):
    return jnp.maximum(x, 0.0) + jnp.log(1.0 + jnp.exp(-jnp.abs(x)))


def _mm_kernel(a_ref, b_ref, o_ref):
    o_ref[...] = _dot(a_ref[...], b_ref[...]).astype(o_ref.dtype)


def _mm(a, b, tm, tn, out_dtype=F32):
    m, k = a.shape
    _, n = b.shape
    assert m % tm == 0 and n % tn == 0, (m, n, tm, tn)
    return pl.pallas_call(
        _mm_kernel,
        grid=(m // tm, n // tn),
        in_specs=[pl.BlockSpec((tm, k), lambda i, j: (i, 0)),
                  pl.BlockSpec((k, tn), lambda i, j: (0, j))],
        out_specs=pl.BlockSpec((tm, tn), lambda i, j: (i, j)),
        out_shape=jax.ShapeDtypeStruct((m, n), out_dtype),
        compiler_params=_cparams(("parallel", "arbitrary")),
        name="mm",
    )(a, b)


REGROUP_HALF = 64
_PAD_HALF = (RW_BASE + RW_COLS) // REGROUP_HALF


def _regroup_src(h):
    gates = (RW_COLS + M_COLS) // REGROUP_HALF
    shift = (Z_BASE - RW_COLS) // REGROUP_HALF
    return jnp.where(h < RW_BASE // REGROUP_HALF, h + gates,
                     jnp.where(h < _PAD_HALF, h - RW_BASE // REGROUP_HALF,
                               jnp.where(h == _PAD_HALF, 0, h - shift)))


def _regroup_kernel(a_ref, b_ref, o_ref):
    c = pl.program_id(0)
    b = jnp.where(2 * c + 1 == _PAD_HALF, 0.0, b_ref[...])
    t = jnp.concatenate([a_ref[...], b], axis=0)
    o_ref[...] = t.T.astype(BF16)


def _regroup(w, layer):
    _, d, n = w.shape
    assert n == RW_COLS + M_COLS + GATE_COLS and _PAD_HALF % 2 == 1
    wt = jnp.swapaxes(w, 1, 2)
    half = lambda k: pl.BlockSpec((None, REGROUP_HALF, d),
                                  lambda c: (layer, _regroup_src(2 * c + k), 0))
    return pl.pallas_call(
        _regroup_kernel,
        grid=(U_COLS // LANES,),
        in_specs=[half(0), half(1)],
        out_specs=pl.BlockSpec((d, LANES), lambda c: (0, c)),
        out_shape=jax.ShapeDtypeStruct((d, U_COLS), BF16),
        compiler_params=_cparams(("parallel",)),
        name="regroup",
    )(wt, wt)


def _rms(t):
    return t * lax.rsqrt(jnp.mean(t * t, axis=-1, keepdims=True) + RMS_EPS)


def _ffn_kernel(h_ref, mod_ref, gpre_ref, gpost_ref, wg_ref, wu_ref, wd_ref, o_ref,
                xn_ref, acc_ref, *, sub, weight):
    j = pl.program_id(1)
    last = pl.num_programs(1) - 1

    def partial_down(xn):
        g = jnp.dot(xn, wg_ref[...], preferred_element_type=F32)
        u = jnp.dot(xn, wu_ref[...], preferred_element_type=F32)
        a = (g * jax.nn.sigmoid(g)) * u
        return jnp.dot(a.astype(BF16), wd_ref[...], preferred_element_type=F32)

    @pl.when(j == 0)
    def _():
        y = _rms(h_ref[...]) * gpre_ref[...]
        xn = y * (1.0 + mod_ref[0, 3 * sub + 1:3 * sub + 2, :]) + mod_ref[0, 3 * sub:3 * sub + 1, :]
        xn = xn.astype(BF16)
        xn_ref[...] = xn
        acc_ref[...] = partial_down(xn)

    @pl.when((j > 0) & (j < last))
    def _():
        acc_ref[...] += partial_down(xn_ref[...])

    @pl.when(j == last)
    def _():
        yn = _rms(acc_ref[...] + partial_down(xn_ref[...])) * gpost_ref[...]
        o_ref[...] = h_ref[...] + (weight * mod_ref[0, 3 * sub + 2:3 * sub + 3, :]) * yn


def _ffn(h, mod, mod_row, sub, g_pre, g_post, wg, wu, wd, tm, tf):
    m, d = h.shape
    f = wg.shape[1]
    assert m % tm == 0 and f % tf == 0 and f // tf >= 2
    return pl.pallas_call(
        functools.partial(_ffn_kernel, sub=sub, weight=MACARON_W),
        grid=(m // tm, f // tf),
        in_specs=[pl.BlockSpec((tm, d), lambda i, j: (i, 0)),
                  pl.BlockSpec((1, N_MOD, d), lambda i, j: (mod_row(i), 0, 0)),
                  pl.BlockSpec((1, d), lambda i, j: (0, 0)),
                  pl.BlockSpec((1, d), lambda i, j: (0, 0)),
                  pl.BlockSpec((d, tf), lambda i, j: (0, j)),
                  pl.BlockSpec((d, tf), lambda i, j: (0, j)),
                  pl.BlockSpec((tf, d), lambda i, j: (j, 0))],
        out_specs=pl.BlockSpec((tm, d), lambda i, j: (i, 0)),
        out_shape=jax.ShapeDtypeStruct((m, d), F32),
        scratch_shapes=[pltpu.VMEM((tm, d), BF16), pltpu.VMEM((tm, d), F32)],
        compiler_params=_cparams(("parallel", "arbitrary")),
        name="ffn",
    )(h, mod, g_pre.reshape(1, d), g_post.reshape(1, d), wg, wu, wd)


def _norm_mm_kernel(*refs, sub, modulated):
    if modulated:
        h_ref, mod_ref, gpre_ref, w_ref, o_ref, xn_ref = refs
    else:
        h_ref, gpre_ref, w_ref, o_ref, xn_ref = refs

    j = pl.program_id(1)

    @pl.when(j == 0)
    def _():
        xn = _rms(h_ref[...]) * gpre_ref[...]
        if modulated:
            xn = xn * (1.0 + mod_ref[0, 3 * sub + 1:3 * sub + 2, :]) + mod_ref[0, 3 * sub:3 * sub + 1, :]
        xn = xn.astype(BF16)
        xn_ref[...] = xn
        o_ref[...] = jnp.dot(xn, w_ref[...], preferred_element_type=F32)

    @pl.when(j > 0)
    def _():
        o_ref[...] = jnp.dot(xn_ref[...], w_ref[...], preferred_element_type=F32)


def _norm_mm(h, mod, mod_row, sub, g_pre, w, tm, tn):
    m, d = h.shape
    n = w.shape[1]
    assert m % tm == 0 and n % tn == 0
    modulated = mod is not None
    mod_specs = [pl.BlockSpec((1, N_MOD, d), lambda i, j: (mod_row(i), 0, 0))] if modulated else []
    return pl.pallas_call(
        functools.partial(_norm_mm_kernel, sub=sub, modulated=modulated),
        grid=(m // tm, n // tn),
        in_specs=[pl.BlockSpec((tm, d), lambda i, j: (i, 0))] + mod_specs
        + [pl.BlockSpec((1, d), lambda i, j: (0, 0)),
           pl.BlockSpec((d, tn), lambda i, j: (0, j))],
        out_specs=pl.BlockSpec((tm, tn), lambda i, j: (i, j)),
        out_shape=jax.ShapeDtypeStruct((m, n), F32),
        scratch_shapes=[pltpu.VMEM((tm, d), BF16)],
        compiler_params=_cparams(("parallel", "arbitrary")),
        name="norm_mm",
    )(*([h] + ([mod] if modulated else []) + [g_pre.reshape(1, d), w]))


def _merge_kernel(br_ref, bm_ref, gr_ref, gm_ref, h_ref, mod_ref, gpost_ref, w_ref, o_ref, *, sub):
    merged = _sigmoid(gr_ref[...]) * br_ref[...] + _sigmoid(gm_ref[...]) * bm_ref[...]
    y = jnp.dot(merged.astype(BF16), w_ref[...], preferred_element_type=F32)
    yn = _rms(y) * gpost_ref[...]
    o_ref[...] = h_ref[...] + mod_ref[0, 3 * sub + 2:3 * sub + 3, :] * yn


def _merge(br, bm, u2d, h, mod, mod_row, sub, g_post, w, tm):
    m, d = h.shape
    tile = pl.BlockSpec((tm, d), lambda i: (i, 0))
    return pl.pallas_call(
        functools.partial(_merge_kernel, sub=sub),
        grid=(m // tm,),
        in_specs=[tile, tile,
                  pl.BlockSpec((tm, d), lambda i: (i, 0)),
                  pl.BlockSpec((tm, d), lambda i: (i, 1)),
                  tile,
                  pl.BlockSpec((1, N_MOD, d), lambda i: (mod_row(i), 0, 0)),
                  pl.BlockSpec((1, d), lambda i: (0, 0)),
                  pl.BlockSpec((d, d), lambda i: (0, 0))],
        out_specs=tile,
        out_shape=jax.ShapeDtypeStruct((m, d), F32),
        compiler_params=_cparams(("parallel",)),
        name="merge",
    )(br, bm, u2d, u2d, h, mod, g_post.reshape(1, d), w)


def _rwkv_feat_kernel(*refs, mode, period, tiles_per_seq, tt):
    if mode == "grid":
        (r_ref, k_ref, v_ref, lo_ref, kprev_ref, vprev_ref, vnext_ref, lonext_ref,
         mur_ref, muk_ref, muv_ref, mulo_ref, wl_ref, pv_ref) = refs[:14]
        rest = refs[14:]
    else:
        (r_ref, k_ref, v_ref, lo_ref, mur_ref, muk_ref, muv_ref, mulo_ref, wl_ref, pv_ref) = refs[:10]
        rest = refs[10:]
    r_o, k_o, v_o, kk_o, b_o, lwf_o, lwb_o, g_o, bonus_o, act_scr = rest
    i = pl.program_id(0)
    j = pl.program_id(1)
    cw = r_ref.shape[1]
    t = lax.broadcasted_iota(jnp.int32, (tt, 1), 0)
    row_first = (t % period) == 0
    row_last = (t % period) == period - 1
    seq_first = (i % tiles_per_seq) == 0
    seq_last = (i % tiles_per_seq) == tiles_per_seq - 1
    quarter = RW_COLS // 4

    def prev1(x):
        return jnp.where(row_first, 0.0, pltpu.roll(x, 1, 0))

    def next1(x):
        return jnp.where(row_last, 0.0, pltpu.roll(x, tt - 1, 0))

    def up(x, halo_ref):
        halo = jnp.where(seq_first, 0.0, halo_ref[...])
        return jnp.concatenate([halo, x[:tt - GRID_W]], axis=0)

    def down(x, halo_ref):
        halo = jnp.where(seq_last, 0.0, halo_ref[...])
        return jnp.concatenate([x[GRID_W:], halo], axis=0)

    def col_index(base, width):
        return base + lax.broadcasted_iota(jnp.int32, (1, width), 1)

    def lerp(x, xs, mu_ref):
        return x + (xs - x) * mu_ref[...]

    @pl.when(j == 0)
    def _():
        x = lo_ref[...]
        xs = down(x, lonext_ref) if mode == "grid" else next1(x)
        x = lerp(x, xs, mulo_ref)
        c = col_index(0, LORA_W)
        act = jnp.where(c < W_LORA, jnp.tanh(x), jnp.where(c < W_LORA + A_LORA, x, _sigmoid(x)))
        act_scr[...] = act.astype(BF16)

    x = r_ref[...]
    c = col_index(j * cw, cw)
    if mode == "grid":
        xs = jnp.where(c < quarter, prev1(x), next1(x))
    else:
        xs = prev1(x)
    r = lerp(x, xs, mur_ref)

    x = k_ref[...]
    c = col_index(RW_DIM + j * cw, cw)
    if mode == "grid":
        xs = jnp.where(c < 2 * quarter, next1(x), up(x, kprev_ref))
    else:
        xs = jnp.where(c < 2 * quarter, prev1(x), next1(x))
    k = lerp(x, xs, muk_ref)

    x = v_ref[...]
    c = col_index(2 * RW_DIM + j * cw, cw)
    if mode == "grid":
        xs = jnp.where(c < 3 * quarter, up(x, vprev_ref), down(x, vnext_ref))
    else:
        xs = next1(x)
    v = lerp(x, xs, muv_ref)

    lo = jnp.dot(act_scr[...], wl_ref[...], preferred_element_type=F32)
    w0f, w0b, a0 = pv_ref[0:1, :], pv_ref[1:2, :], pv_ref[2:3, :]
    k_k, k_a, r_k = pv_ref[3:4, :], pv_ref[4:5, :], pv_ref[5:6, :]
    a = _sigmoid(a0 + lo[:, 2 * cw:3 * cw])
    ones = _group_ones(cw, RW_HEAD)
    kkr = k * k_k
    kk = kkr * lax.rsqrt(jnp.maximum(_group_sum(kkr * kkr, ones), L2_EPS * L2_EPS))
    k = k * (1.0 + (a - 1.0) * k_a)
    lwf_o[...] = -math.exp(-0.5) * _sigmoid(w0f + lo[:, 0:cw])
    lwb_o[...] = -math.exp(-0.5) * _sigmoid(w0b + lo[:, cw:2 * cw])
    r_o[...] = r.astype(r_o.dtype)
    k_o[...] = k.astype(k_o.dtype)
    v_o[...] = v.astype(v_o.dtype)
    kk_o[...] = kk.astype(kk_o.dtype)
    b_o[...] = (kk * a).astype(b_o.dtype)
    g_o[...] = lo[:, 3 * cw:4 * cw]
    bonus_o[...] = _group_sum(r * k * r_k, ones) * v


def _rwkv_feat(u2d, mode, seq_len, tt, mu, w_lora, pv):
    m = u2d.shape[0]
    cw = FEAT_CW
    ncb = RW_DIM // cw
    assert m % tt == 0 and seq_len % tt == 0
    tiles_per_seq = seq_len // tt
    rb, kb, vb = RW_BASE // cw, (RW_BASE + RW_DIM) // cw, (RW_BASE + 2 * RW_DIM) // cw
    lob = LORA_BASE // LORA_W
    hr = tt // GRID_W
    nhb = m // GRID_W
    main = [pl.BlockSpec((tt, cw), lambda i, j: (i, rb + j)),
            pl.BlockSpec((tt, cw), lambda i, j: (i, kb + j)),
            pl.BlockSpec((tt, cw), lambda i, j: (i, vb + j)),
            pl.BlockSpec((tt, LORA_W), lambda i, j: (i, lob))]
    args = [u2d, u2d, u2d, u2d]
    if mode == "grid":
        assert tt % GRID_W == 0
        period = GRID_W
        prev_row = lambda i: jnp.maximum(i * hr - 1, 0)
        next_row = lambda i: jnp.minimum((i + 1) * hr, nhb - 1)
        main += [pl.BlockSpec((GRID_W, cw), lambda i, j: (prev_row(i), kb + j)),
                 pl.BlockSpec((GRID_W, cw), lambda i, j: (prev_row(i), vb + j)),
                 pl.BlockSpec((GRID_W, cw), lambda i, j: (next_row(i), vb + j)),
                 pl.BlockSpec((GRID_W, LORA_W), lambda i, j: (next_row(i), lob))]
        args += [u2d, u2d, u2d, u2d]
    else:
        assert tiles_per_seq == 1
        period = tt
    params = [pl.BlockSpec((1, cw), lambda i, j: (0, j)),
              pl.BlockSpec((1, cw), lambda i, j: (0, RW_DIM // cw + j)),
              pl.BlockSpec((1, cw), lambda i, j: (0, 2 * RW_DIM // cw + j)),
              pl.BlockSpec((1, LORA_W), lambda i, j: (0, 3 * RW_DIM // LORA_W)),
              pl.BlockSpec((None, LORA_W, 4 * cw), lambda i, j: (j, 0, 0)),
              pl.BlockSpec((SUBLANES, cw), lambda i, j: (0, j))]
    args += [mu, mu, mu, mu, w_lora, pv]
    out_spec = pl.BlockSpec((tt, cw), lambda i, j: (i, j))
    return pl.pallas_call(
        functools.partial(_rwkv_feat_kernel, mode=mode, period=period,
                          tiles_per_seq=tiles_per_seq, tt=tt),
        grid=(m // tt, ncb),
        in_specs=main + params,
        out_specs=[out_spec] * 9,
        out_shape=[jax.ShapeDtypeStruct((m, RW_DIM), BF16)] * 5
        + [jax.ShapeDtypeStruct((m, RW_DIM), F32)] * 4,
        scratch_shapes=[pltpu.VMEM((tt, LORA_W), BF16)],
        compiler_params=_cparams(("parallel", "arbitrary")),
        name="rwkv_feat",
    )(*args)


def _conv_kernel(x_ref, prev_ref, next_ref, w_ref, o_ref, *, tiles_per_seq, tt):
    i = pl.program_id(0)
    x = x_ref[...]
    t = lax.broadcasted_iota(jnp.int32, (tt, 1), 0)
    seq_first = (i % tiles_per_seq) == 0
    seq_last = (i % tiles_per_seq) == tiles_per_seq - 1
    hp = jnp.where(seq_first, 0.0, prev_ref[SUBLANES - 1:SUBLANES, :])
    hn = jnp.where(seq_last, 0.0, next_ref[0:1, :])
    prev = jnp.where(t == 0, hp, pltpu.roll(x, 1, 0))
    nxt = jnp.where(t == tt - 1, hn, pltpu.roll(x, tt - 1, 0))
    y = prev * w_ref[0:1, :] + x * w_ref[1:2, :] + nxt * w_ref[2:3, :] + w_ref[3:4, :]
    o_ref[...] = y * _sigmoid(y)


def _conv(u2d, seq_len, tt, wconv):
    m = u2d.shape[0]
    cw = CONV_CW
    base = XBC_BASE // cw
    tiles_per_seq = seq_len // tt
    hr = tt // SUBLANES
    nhb = m // SUBLANES
    return pl.pallas_call(
        functools.partial(_conv_kernel, tiles_per_seq=tiles_per_seq, tt=tt),
        grid=(m // tt, M_XBC // cw),
        in_specs=[pl.BlockSpec((tt, cw), lambda i, j: (i, base + j)),
                  pl.BlockSpec((SUBLANES, cw), lambda i, j: (jnp.maximum(i * hr - 1, 0), base + j)),
                  pl.BlockSpec((SUBLANES, cw),
                               lambda i, j: (jnp.minimum((i + 1) * hr, nhb - 1), base + j)),
                  pl.BlockSpec((SUBLANES, cw), lambda i, j: (0, j))],
        out_specs=pl.BlockSpec((tt, cw), lambda i, j: (i, j)),
        out_shape=jax.ShapeDtypeStruct((m, M_XBC), F32),
        compiler_params=_cparams(("parallel", "parallel")),
        name="conv",
    )(u2d, u2d, u2d, wconv)


def _stack2(x, m0):
    return jnp.concatenate([jnp.where(m0, x, 0.0), jnp.where(m0, 0.0, x)], axis=0)


def _rwkv_kernel(*refs, nchunks, pp, reverse, emit, finish):
    r_ref, k_ref, v_ref, kk_ref, b_ref, lw_ref, s0_ref = refs[:7]
    rest = list(refs[7:])
    if finish:
        yacc_ref, bonus_ref, g_ref, lng_ref, lnb_ref = rest[:5]
        rest = rest[5:]
    y_ref = rest.pop(0) if emit else None
    sfin_ref, s_scr = rest
    i = pl.program_id(2)

    @pl.when(i == 0)
    def _():
        s_scr[...] = s0_ref[...]

    lane = lax.broadcasted_iota(jnp.int32, (CHUNK, LANES), 1)
    trow = lax.broadcasted_iota(jnp.int32, (CHUNK, LANES), 0)
    m0 = lane < RW_HEAD
    scol = lane % RW_HEAD
    strict = (scol > trow) if reverse else (scol < trow)
    incl = (scol >= trow) if reverse else (scol <= trow)
    eye = (scol == trow).astype(F32)
    row2 = lax.broadcasted_iota(jnp.int32, (LANES, LANES), 0)
    col2 = lax.broadcasted_iota(jnp.int32, (LANES, LANES), 1)
    same = (row2 // RW_HEAD) == (col2 // RW_HEAD)
    tri3 = _tri3(reverse)
    last = 0 if reverse else CHUNK - 1
    bf = lambda x: x.astype(BF16)
    st2 = lambda x: _stack2(x, m0)

    sls = [(slice(c * CHUNK, (c + 1) * CHUNK), slice(hp * LANES, (hp + 1) * LANES))
           for c in range(nchunks) for hp in range(pp)]
    G = range(nchunks * pp)
    lw = [lw_ref[sl] for sl in sls]
    cum = [_dot_01_lhs(tri3, x) for x in lw]
    tot = [c[last:last + 1, :] for c in cum]
    eneg = [jnp.exp(-c) for c in cum]
    at = [bf(-kk_ref[sl] * jnp.exp(c - x)) for sl, c, x in zip(sls, cum, lw)]
    bk2 = [jnp.concatenate([st2(bf(b_ref[sl] * e)), st2(bf(k_ref[sl] * e))], axis=0)
           for sl, e in zip(sls, eneg)]
    if emit:
        rt = [r_ref[sl] * jnp.exp(c) for sl, c in zip(sls, cum)]
        big = [_dot_nt(jnp.concatenate([x, bf(y)], axis=0), z) for x, y, z in zip(at, rt, bk2)]
        rb = [bf(jnp.where(incl, x[CHUNK:, :LANES], 0.0)) for x in big]
        rk = [bf(jnp.where(incl, x[CHUNK:, LANES:], 0.0)) for x in big]
    else:
        big = [_dot_nt(x, z) for x, z in zip(at, bk2)]
    n = [jnp.where(strict, x[:CHUNK, :LANES], 0.0) for x in big]
    ak = [bf(jnp.where(strict, x[:CHUNK, LANES:], 0.0)) for x in big]
    v = [bf(v_ref[sl]) for sl in sls]
    v2 = [st2(x) for x in v]
    akv = [bf(_dot(x, y)) for x, y in zip(ak, v2)]
    p = [eye + x for x in n]
    npow = [bf(x) for x in n]
    npow = [bf(_dot(x, st2(x))) for x in npow]
    for it in range(4):
        both = [_dot(jnp.concatenate([x, bf(y)], axis=0), st2(x)) for x, y in zip(npow, p)]
        p = [y + x[CHUNK:] for x, y in zip(both, p)]
        npow = [bf(x[:CHUNK]) for x in both]
    p = [bf(y + _dot(bf(y), st2(x))) for x, y in zip(npow, p)]
    wu = [_dot(x, jnp.concatenate([st2(y), st2(z)], axis=1)) for x, y, z in zip(p, at, akv)]
    w = [bf(x[:, :LANES]) for x in wu]
    up = [bf(x[:, LANES:]) for x in wu]
    eend = [jnp.exp(t - c) for t, c in zip(tot, cum)]
    bg = [bf(b_ref[sl] * e) for sl, e in zip(sls, eend)]
    kg = [bf(k_ref[sl] * e) for sl, e in zip(sls, eend)]
    mc = [bf(jnp.where(same, _dot_tn(x, y), 0.0)) for x, y in zip(w, bg)]
    nc = [jnp.where(same, _dot_tn(jnp.concatenate([x, y], axis=0),
                                  jnp.concatenate([z, t], axis=0)), 0.0)
          for x, y, z, t in zip(up, v, bg, kg)]
    gam = [jnp.exp(t) for t in tot]
    if emit:
        q = [bf(x + _dot(y, st2(z))) for x, y, z in zip(rt, rb, w)]
        yp = [_dot(jnp.concatenate([x, y], axis=1), jnp.concatenate([st2(z), t], axis=0))
              for x, y, z, t in zip(rb, rk, up, v2)]

    s = [s_scr[hp] for hp in range(pp)]
    ys = [None] * len(G)
    for c in (reversed(range(nchunks)) if reverse else range(nchunks)):
        for hp in range(pp):
            g = c * pp + hp
            sb = s[hp].astype(BF16)
            if emit:
                ys[g] = _dot_nt(q[g], sb) + yp[g]
            s[hp] = s[hp] * gam[g] + _dot(sb, mc[g]) + nc[g]
    for hp in range(pp):
        s_scr[hp] = s[hp]

    if emit and not finish:
        for g in G:
            y_ref[sls[g]] = ys[g]
    if finish:
        ones = _group_ones(LANES, RW_HEAD)
        ysum = [y + yacc_ref[sl] for y, sl in zip(ys, sls)]
        mean = [_group_sum(y, ones) * (1.0 / RW_HEAD) for y in ysum]
        dev = [y - mu for y, mu in zip(ysum, mean)]
        var = [_group_sum(d * d, ones) * (1.0 / RW_HEAD) for d in dev]
        for g in G:
            yn = dev[g] * lax.rsqrt(var[g] + GN_EPS)
            lanes = sls[g][1]
            o = (yn * lng_ref[:, lanes] + lnb_ref[:, lanes] + bonus_ref[sls[g]]) * g_ref[sls[g]]
            y_ref[sls[g]] = o.astype(y_ref.dtype)

    @pl.when(i == pl.num_programs(2) - 1)
    def _():
        for hp in range(pp):
            sfin_ref[hp] = s[hp]


RW_PAIRS_PER_STEP = 4


def _rwkv_scan(r, k, v, kk, b, lw, s0, tb, reverse, emit=True, finish=None):
    bsz, l, _ = r.shape
    pp = RW_PAIRS_PER_STEP
    npair = RW_DIM // LANES
    nblk = l // tb
    assert l % tb == 0 and tb % CHUNK == 0 and npair % pp == 0
    blk = (lambda i: nblk - 1 - i) if reverse else (lambda i: i)
    seq = pl.BlockSpec((None, tb, pp * LANES), lambda bi, p, i: (bi, blk(i), p))
    st = pl.BlockSpec((None, pp, LANES, LANES), lambda bi, p, i: (bi, p, 0, 0))
    vec = pl.BlockSpec((1, pp * LANES), lambda bi, p, i: (0, p))
    st_shape = jax.ShapeDtypeStruct((bsz, npair, LANES, LANES), F32)
    y_shape = jax.ShapeDtypeStruct((bsz, l, RW_DIM), BF16 if finish else F32)
    args = [r, k, v, kk, b, lw, s0]
    in_specs = [seq] * 6 + [st]
    if finish:
        y_other, bonus, g, ln_g, ln_b = finish
        args += [y_other, bonus, g, ln_g.reshape(1, RW_DIM), ln_b.reshape(1, RW_DIM)]
        in_specs += [seq, seq, seq, vec, vec]
    outs = pl.pallas_call(
        functools.partial(_rwkv_kernel, nchunks=tb // CHUNK, pp=pp, reverse=reverse, emit=emit,
                          finish=bool(finish)),
        grid=(bsz, npair // pp, nblk),
        in_specs=in_specs,
        out_specs=([seq] if emit else []) + [st],
        out_shape=([y_shape] if emit else []) + [st_shape],
        scratch_shapes=[pltpu.VMEM((pp, LANES, LANES), F32)],
        compiler_params=_cparams(("parallel", "parallel", "arbitrary")),
        name="rwkv_scan",
    )(*args)
    return outs if emit else (None, outs[0])


def _ssd_kernel(*refs, nchunks, reverse, emit, finish):
    x_ref, b_ref, c_ref, udt_ref, bias_ref, aall_ref, s0_ref = refs[:7]
    rest = list(refs[7:])
    if finish:
        yacc_ref, z_ref, dskip_ref = rest[:3]
        rest = rest[3:]
    y_ref = rest.pop(0) if emit else None
    sfin_ref, s_scr = rest
    grp = pl.program_id(1)
    i = pl.program_id(2)

    @pl.when(i == 0)
    def _():
        s_scr[...] = s0_ref[...]

    width = M_HPG * M_HEADDIM
    m0 = lax.broadcasted_iota(jnp.int32, (CHUNK, LANES), 1) < M_HEADDIM
    tri3 = _tri3(reverse)
    qr = lax.broadcasted_iota(jnp.int32, (3 * CHUNK, CHUNK), 0) % CHUNK
    qc = lax.broadcasted_iota(jnp.int32, (3 * CHUNK, CHUNK), 1)
    tri3_t = ((qr >= qc) if reverse else (qr <= qc)).astype(BF16)
    head0 = (M_HEADS if reverse else 0) + grp * M_HPG
    hr = lax.broadcasted_iota(jnp.int32, (3 * LANES, M_HPG), 0) % LANES
    hc = lax.broadcasted_iota(jnp.int32, (3 * LANES, M_HPG), 1)
    sel3 = (hr == head0 + hc).astype(BF16)
    er = lax.broadcasted_iota(jnp.int32, (3 * M_HPG, width), 0) % M_HPG
    ec = lax.broadcasted_iota(jnp.int32, (3 * M_HPG, width), 1) // M_HEADDIM
    expand3 = (er == ec).astype(BF16)
    lt = lax.broadcasted_iota(jnp.int32, (CHUNK, LANES), 0)
    ll = lax.broadcasted_iota(jnp.int32, (CHUNK, LANES), 1)
    keep = ((ll % CHUNK) >= lt) if reverse else ((ll % CHUNK) <= lt)
    last = 0 if reverse else CHUNK - 1
    split_cat = lambda x, axis: jnp.concatenate(_split3(x), axis=axis)

    G = range(nchunks)
    sls = [slice(g * CHUNK, (g + 1) * CHUNK) for g in G]
    dt_all = [_softplus(udt_ref[sl] + bias_ref[...]) for sl in sls]
    both = [jnp.dot(split_cat(jnp.concatenate([x, x * aall_ref[...]], axis=0), 1), sel3,
                    preferred_element_type=F32) for x in dt_all]
    dt = [x[:CHUNK] for x in both]
    da = [x[CHUNK:] for x in both]
    cum = [_dot_01_lhs(tri3, x) for x in da]
    dt_x = [jnp.dot(split_cat(x, 1), expand3, preferred_element_type=F32) for x in dt]
    cum_x = [jnp.dot(split_cat(x, 1), expand3, preferred_element_type=F32) for x in cum]
    tot_x = [x[last:last + 1, :] for x in cum_x]
    xdt = [x_ref[sl] * y for sl, y in zip(sls, dt_x)]
    xend = [x * jnp.exp(t - c) for x, t, c in zip(xdt, tot_x, cum_x)]
    bm = [b_ref[sl].astype(BF16) for sl in sls]
    cs = [_dot_tn(x, y) for x, y in zip(bm, xend)]
    gam = [jnp.exp(t) for t in tot_x]
    if emit:
        cm = [c_ref[sl].astype(BF16) for sl in sls]
        cum_t = [lax.dot_general(split_cat(x, 0), tri3_t, (((0,), (0,)), ((), ())),
                                 preferred_element_type=F32) for x in da]
        cb2 = [_dot_nt(x, jnp.concatenate([y, y], axis=0)) for x, y in zip(cm, bm)]
        ydiag = []
        for g in G:
            ys = []
            for j in range(M_HPG // 2):
                colsel = cum_x[g][:, j * LANES:(j + 1) * LANES]
                rowsel = jnp.concatenate([cum_t[g][2 * j:2 * j + 1, :],
                                          cum_t[g][2 * j + 1:2 * j + 2, :]], axis=1)
                seg = jnp.where(keep, colsel - rowsel, -jnp.inf)
                gp = cb2[g] * jnp.exp(seg)
                xp = xdt[g][:, j * LANES:(j + 1) * LANES]
                ys.append(_dot(gp, _stack2(xp, m0)))
            ydiag.append(jnp.concatenate(ys, axis=1))
        ecum = [jnp.exp(x) for x in cum_x]

    s = s_scr[...]
    for g in (reversed(G) if reverse else G):
        if emit:
            y = _dot(cm[g], s) * ecum[g] + ydiag[g]
            if finish:
                zz = z_ref[sls[g]]
                y = (y + yacc_ref[sls[g]] + dskip_ref[...] * x_ref[sls[g]]) * (zz * _sigmoid(zz))
            y_ref[sls[g]] = y
        s = s * gam[g] + cs[g]
    s_scr[...] = s

    @pl.when(i == pl.num_programs(2) - 1)
    def _():
        sfin_ref[...] = s


def _ssd_scan(xbc, u, dt_bias, a_all, s0, tb, reverse, emit=True, finish=None):
    bsz, l, _ = xbc.shape
    ncb = tb // CHUNK
    nblk = l // tb
    width = M_HPG * M_HEADDIM
    blk = (lambda i: nblk - 1 - i) if reverse else (lambda i: i)
    st = pl.BlockSpec((None, None, M_STATE, width), lambda bi, g, i: (bi, g, 0, 0))
    xspec = pl.BlockSpec((None, tb, width), lambda bi, g, i: (bi, blk(i), g))
    bspec = pl.BlockSpec((None, tb, M_STATE), lambda bi, g, i: (bi, blk(i), M_DIM // M_STATE + g))
    cspec = pl.BlockSpec((None, tb, M_STATE),
                         lambda bi, g, i: (bi, blk(i), (M_DIM + M_BC) // M_STATE + g))
    vec = pl.BlockSpec((1, 2 * M_HEADS), lambda bi, g, i: (0, 0))
    args = [xbc, xbc, xbc, u, dt_bias, a_all, s0]
    in_specs = [xspec, bspec, cspec,
                pl.BlockSpec((None, tb, 2 * M_HEADS), lambda bi, g, i: (bi, blk(i), DT_BASE // (2 * M_HEADS))),
                vec, vec, st]
    aliases = {}
    if finish:
        y_other, dskip = finish
        args += [y_other, u, dskip]
        in_specs += [xspec,
                     pl.BlockSpec((None, tb, width), lambda bi, g, i: (bi, blk(i), Z_BASE // width + g)),
                     pl.BlockSpec((1, width), lambda bi, g, i: (0, g))]
        aliases = {7: 0}
    outs = pl.pallas_call(
        functools.partial(_ssd_kernel, nchunks=ncb, reverse=reverse, emit=emit,
                          finish=bool(finish)),
        grid=(bsz, M_GROUPS, nblk),
        in_specs=in_specs,
        out_specs=([xspec] if emit else []) + [st],
        out_shape=([jax.ShapeDtypeStruct((bsz, l, M_DIM), F32)] if emit else [])
        + [jax.ShapeDtypeStruct((bsz, M_GROUPS, M_STATE, width), F32)],
        scratch_shapes=[pltpu.VMEM((M_STATE, width), F32)],
        input_output_aliases=aliases,
        compiler_params=_cparams(("parallel", "parallel", "arbitrary")),
        name="ssd_scan",
    )(*args)
    return outs if emit else (None, outs[0])


def kernel(x, c, ctx, c_ctx, ada_w, ada_b, norm_pre, norm_post, ffn1_gate, ffn1_up, ffn1_down,
           ffn2_gate, ffn2_up, ffn2_down, w_in, rw_mu, rw_w0, rw_w_up, rw_a0, rw_a_up, rw_g_up,
           rw_k_k, rw_k_a, rw_r_k, rw_ln_g, rw_ln_b, m_conv_w, m_conv_b, m_dt_bias, m_A_log, m_D,
           m_norm, w_br, w_bm, w_out):
    bsz, l, d = x.shape
    lc = ctx.shape[1]
    i = 0
    tm = 512
    tpb = l // tm

    cc = jnp.concatenate([c, c_ctx[None], jnp.zeros((SUBLANES - bsz - 1, d), F32)], axis=0)
    mod = _mm(jax.nn.silu(cc), ada_w[i], SUBLANES, 1024) + ada_b[i]
    mod = mod.reshape(SUBLANES, N_MOD, d)
    lat_row = lambda t: t // tpb
    ctx_row = lambda t: bsz

    bf = lambda w: w.astype(BF16)
    h = x.reshape(bsz * l, d)
    hc = ctx.reshape(bsz * lc, d)
    wg1, wu1, wd1 = bf(ffn1_gate[i]), bf(ffn1_up[i]), bf(ffn1_down[i])
    h = _ffn(h, mod, lat_row, 0, norm_pre[i, 0], norm_post[i, 0], wg1, wu1, wd1, tm, 512)
    hc = _ffn(hc, mod, ctx_row, 0, norm_pre[i, 0], norm_post[i, 0], wg1, wu1, wd1, tm, 512)

    w_cat = _regroup(w_in, i)
    tn = 1920
    u2d = _norm_mm(h, mod, lat_row, 1, norm_pre[i, 1], w_cat, tm, tn)
    uc2d = _norm_mm(hc, mod, ctx_row, 1, norm_pre[i, 1], w_cat, tm, tn)
    u = u2d.reshape(bsz, l, U_COLS)
    uc = uc2d.reshape(bsz, lc, U_COLS)

    cw = FEAT_CW
    mu = jnp.pad(rw_mu[i], (0, 3 * RW_DIM + LORA_W - RW_COLS)).reshape(1, -1)
    zero = lambda rows: jnp.zeros((rows, RW_DIM), F32)
    wl_rows = lambda w, before, after: jnp.concatenate([zero(before), w, zero(after)], axis=0)
    w_lora = jnp.stack([wl_rows(rw_w_up[i, 0], 0, LORA_W - W_LORA),
                        wl_rows(rw_w_up[i, 1], 0, LORA_W - W_LORA),
                        wl_rows(rw_a_up[i], W_LORA, LORA_W - W_LORA - A_LORA),
                        wl_rows(rw_g_up[i], W_LORA + A_LORA, LORA_W - W_LORA - A_LORA - G_LORA)],
                       axis=1)
    w_lora = (w_lora.reshape(LORA_W, 4, RW_DIM // cw, cw).transpose(2, 0, 1, 3)
              .reshape(RW_DIM // cw, LORA_W, 4 * cw).astype(BF16))
    pv = jnp.stack([rw_w0[i, 0], rw_w0[i, 1], rw_a0[i], rw_k_k[i], rw_k_a[i],
                    rw_r_k[i].reshape(RW_DIM), jnp.zeros((RW_DIM,), F32),
                    jnp.zeros((RW_DIM,), F32)], axis=0)
    seq3 = lambda t, n: t.reshape(bsz, n, RW_DIM)
    r, k, v, kk, bvec, lwf, lwb, g, bonus = [
        seq3(t, l) for t in _rwkv_feat(u2d, "grid", l, tm, mu, w_lora, pv)]
    rc, kc, vc, kkc, bc, lwfc, lwbc, _, _ = [
        seq3(t, lc) for t in _rwkv_feat(uc2d, "seq", lc, lc, mu, w_lora, pv)]
    s0r = jnp.zeros((bsz, RW_DIM // LANES, LANES, LANES), F32)
    _, src_f = _rwkv_scan(rc, kc, vc, kkc, bc, lwfc, s0r, lc, False, emit=False)
    _, src_b = _rwkv_scan(rc, kc, vc, kkc, bc, lwbc, s0r, lc, True, emit=False)
    yr, _ = _rwkv_scan(r, k, v, kk, bvec, lwf, src_f, tm, False)
    o_r, _ = _rwkv_scan(r, k, v, kk, bvec, lwb, src_b, tm, True,
                        finish=(yr, bonus, g, rw_ln_g[i], rw_ln_b[i]))
    br = _mm(o_r.reshape(bsz * l, RW_DIM), bf(w_br[i]), tm, D_MODEL)

    wconv = jnp.concatenate([m_conv_w[i], m_conv_b[i][None],
                             jnp.zeros((SUBLANES - 4, M_XBC), F32)], axis=0)
    xbc = _conv(u2d, l, 2 * tm, wconv).reshape(bsz, l, M_XBC)
    xbcc = _conv(uc2d, lc, lc, wconv).reshape(bsz, lc, M_XBC)
    dt_bias = m_dt_bias[i].reshape(1, 2 * M_HEADS)
    a_all = -jnp.exp(m_A_log[i]).reshape(1, 2 * M_HEADS)
    s0m = jnp.zeros((bsz, M_GROUPS, M_STATE, M_HPG * M_HEADDIM), F32)
    _, smc_f = _ssd_scan(xbcc, uc, dt_bias, a_all, s0m, lc, False, emit=False)
    _, smc_b = _ssd_scan(xbcc, uc, dt_bias, a_all, s0m, lc, True, emit=False)
    ym, _ = _ssd_scan(xbc, u, dt_bias, a_all, smc_f, tm, False)
    dskip = jnp.repeat(m_D[i], M_HEADDIM).reshape(1, M_DIM)
    ym, _ = _ssd_scan(xbc, u, dt_bias, a_all, smc_b, tm, True, finish=(ym, dskip))
    bmo = _norm_mm(ym.reshape(bsz * l, M_DIM), None, None, 0, m_norm[i], bf(w_bm[i]), tm, 512)

    h = _merge(br, bmo, u2d, h, mod, lambda t: t // (l // 256), 1, norm_post[i, 1], bf(w_out[i]), 256)
    h = _ffn(h, mod, lat_row, 2, norm_pre[i, 2], norm_post[i, 2], bf(ffn2_gate[i]),
             bf(ffn2_up[i]), bf(ffn2_down[i]), tm, 512)
    return h.reshape(bsz, l, d)
```

```python
import functools
import math

import jax
import jax.numpy as jnp
from jax import lax
from jax.experimental import pallas as pl
from jax.experimental.pallas import tpu as pltpu

F32 = jnp.float32
BF16 = jnp.bfloat16

D_MODEL = 2048
GRID_W = 64
N_MOD = 9
MACARON_W = 0.5
D_FF = 5632
RMS_EPS = 1e-6
GN_EPS = 64e-5
L2_EPS = 1e-12

RW_HEAD = 64
RW_HEADS = D_MODEL // RW_HEAD
RW_DIM = RW_HEADS * RW_HEAD
W_LORA = 96
A_LORA = 96
G_LORA = 256
RW_COLS = 3 * RW_DIM + W_LORA + A_LORA + G_LORA

M_DIM = 2 * D_MODEL
M_HEADDIM = 64
M_HEADS = M_DIM // M_HEADDIM
M_GROUPS = 8
M_HPG = M_HEADS // M_GROUPS
M_STATE = 128
M_BC = M_GROUPS * M_STATE
M_XBC = M_DIM + 2 * M_BC
M_COLS = M_DIM + M_XBC + 2 * M_HEADS
GATE_COLS = 2 * D_MODEL

CHUNK = 64
LANES = 128
SUBLANES = 8
VMEM_LIMIT = 48 * 1024 * 1024

LORA_W = 512
RW_BASE = GATE_COLS
LORA_BASE = RW_BASE + 3 * RW_DIM
Z_BASE = LORA_BASE + LORA_W
XBC_BASE = Z_BASE + M_DIM
DT_BASE = XBC_BASE + M_XBC
U_COLS = DT_BASE + 2 * M_HEADS
FEAT_CW = 512
CONV_CW = 512


def _cparams(sem):
    return pltpu.CompilerParams(dimension_semantics=sem, vmem_limit_bytes=VMEM_LIMIT)


def _dot(a, b):
    return jnp.dot(a.astype(BF16), b.astype(BF16), preferred_element_type=F32)


def _dot_nt(a, b):
    return lax.dot_general(a.astype(BF16), b.astype(BF16), (((1,), (1,)), ((), ())),
                           preferred_element_type=F32)


def _dot_tn(a, b):
    return lax.dot_general(a.astype(BF16), b.astype(BF16), (((0,), (0,)), ((), ())),
                           preferred_element_type=F32)


def _split3(x):
    hi = x.astype(BF16)
    r1 = x - hi.astype(F32)
    mid = r1.astype(BF16)
    lo = (r1 - mid.astype(F32)).astype(BF16)
    return hi, mid, lo


def _dot_01_lhs(t3, x):
    return jnp.dot(t3, jnp.concatenate(_split3(x), axis=0), preferred_element_type=F32)


def _tri3(reverse):
    r = lax.broadcasted_iota(jnp.int32, (CHUNK, 3 * CHUNK), 0)
    c = lax.broadcasted_iota(jnp.int32, (CHUNK, 3 * CHUNK), 1) % CHUNK
    return ((r <= c) if reverse else (r >= c)).astype(BF16)


def _dot_01_rhs(x, e):
    hi, mid, lo = _split3(x)
    return (jnp.dot(hi, e, preferred_element_type=F32) + jnp.dot(mid, e, preferred_element_type=F32)
            + jnp.dot(lo, e, preferred_element_type=F32))


def _group_ones(width, group):
    r = lax.broadcasted_iota(jnp.int32, (width, width), 0)
    c = lax.broadcasted_iota(jnp.int32, (width, width), 1)
    return ((r // group) == (c // group)).astype(BF16)


def _group_sum(x, ones):
    hi = x.astype(BF16)
    lo = (x - hi.astype(F32)).astype(BF16)
    return (jnp.dot(hi, ones, preferred_element_type=F32)
            + jnp.dot(lo, ones, preferred_element_type=F32))


def _sigmoid(x):
    return 0.5 * jnp.tanh(0.5 * x) + 0.5


def _softplus(x):
    return jnp.maximum(x, 0.0) + jnp.log(1.0 + jnp.exp(-jnp.abs(x)))


def _mm_kernel(a_ref, b_ref, o_ref):
    o_ref[...] = _dot(a_ref[...], b_ref[...]).astype(o_ref.dtype)


def _mm(a, b, tm, tn, out_dtype=F32):
    m, k = a.shape
    _, n = b.shape
    assert m % tm == 0 and n % tn == 0, (m, n, tm, tn)
    return pl.pallas_call(
        _mm_kernel,
        grid=(m // tm, n // tn),
        in_specs=[pl.BlockSpec((tm, k), lambda i, j: (i, 0)),
                  pl.BlockSpec((k, tn), lambda i, j: (0, j))],
        out_specs=pl.BlockSpec((tm, tn), lambda i, j: (i, j)),
        out_shape=jax.ShapeDtypeStruct((m, n), out_dtype),
        compiler_params=_cparams(("parallel", "arbitrary")),
        name="mm",
    )(a, b)


REGROUP_HALF = 64
_PAD_HALF = (RW_BASE + RW_COLS) // REGROUP_HALF


def _regroup_src(h):
    gates = (RW_COLS + M_COLS) // REGROUP_HALF
    shift = (Z_BASE - RW_COLS) // REGROUP_HALF
    return jnp.where(h < RW_BASE // REGROUP_HALF, h + gates,
                     jnp.where(h < _PAD_HALF, h - RW_BASE // REGROUP_HALF,
                               jnp.where(h == _PAD_HALF, 0, h - shift)))


REGROUP_HALVES = 6


def _regroup_kernel(*refs):
    o_ref = refs[-1]
    c = pl.program_id(0)
    parts = [jnp.where(REGROUP_HALVES * c + k == _PAD_HALF, 0.0, r[...])
             for k, r in enumerate(refs[:-1])]
    t = jnp.concatenate(parts, axis=0)
    o_ref[...] = t.T.astype(BF16)


def _regroup(w, layer):
    _, d, n = w.shape
    nh = REGROUP_HALVES
    assert n == RW_COLS + M_COLS + GATE_COLS and U_COLS % (nh * REGROUP_HALF) == 0
    wt = jnp.swapaxes(w, 1, 2)
    half = lambda k: pl.BlockSpec((None, REGROUP_HALF, d),
                                  lambda c: (layer, _regroup_src(nh * c + k), 0))
    return pl.pallas_call(
        _regroup_kernel,
        grid=(U_COLS // (nh * REGROUP_HALF),),
        in_specs=[half(k) for k in range(nh)],
        out_specs=pl.BlockSpec((d, nh * REGROUP_HALF), lambda c: (0, c)),
        out_shape=jax.ShapeDtypeStruct((d, U_COLS), BF16),
        compiler_params=_cparams(("parallel",)),
        name="regroup",
    )(*([wt] * nh))


def _rms(t):
    return t * lax.rsqrt(jnp.mean(t * t, axis=-1, keepdims=True) + RMS_EPS)


def _ffn_kernel(h_ref, mod_ref, gpre_ref, gpost_ref, wg_ref, wu_ref, wd_ref, o_ref,
                xn_ref, acc_ref, *, sub, weight):
    j = pl.program_id(1)
    last = pl.num_programs(1) - 1

    def partial_down(xn):
        g = jnp.dot(xn, wg_ref[...], preferred_element_type=F32)
        u = jnp.dot(xn, wu_ref[...], preferred_element_type=F32)
        a = (g * jax.nn.sigmoid(g)) * u
        return jnp.dot(a.astype(BF16), wd_ref[...], preferred_element_type=F32)

    @pl.when(j == 0)
    def _():
        y = _rms(h_ref[...]) * gpre_ref[...]
        xn = y * (1.0 + mod_ref[0, 3 * sub + 1:3 * sub + 2, :]) + mod_ref[0, 3 * sub:3 * sub + 1, :]
        xn = xn.astype(BF16)
        xn_ref[...] = xn
        acc_ref[...] = partial_down(xn)

    @pl.when((j > 0) & (j < last))
    def _():
        acc_ref[...] += partial_down(xn_ref[...])

    @pl.when(j == last)
    def _():
        yn = _rms(acc_ref[...] + partial_down(xn_ref[...])) * gpost_ref[...]
        o_ref[...] = h_ref[...] + (weight * mod_ref[0, 3 * sub + 2:3 * sub + 3, :]) * yn


def _ffn(h, mod, mod_row, sub, g_pre, g_post, wg, wu, wd, tm, tf):
    m, d = h.shape
    f = wg.shape[1]
    assert m % tm == 0 and f % tf == 0 and f // tf >= 2
    return pl.pallas_call(
        functools.partial(_ffn_kernel, sub=sub, weight=MACARON_W),
        grid=(m // tm, f // tf),
        in_specs=[pl.BlockSpec((tm, d), lambda i, j: (i, 0)),
                  pl.BlockSpec((1, N_MOD, d), lambda i, j: (mod_row(i), 0, 0)),
                  pl.BlockSpec((1, d), lambda i, j: (0, 0)),
                  pl.BlockSpec((1, d), lambda i, j: (0, 0)),
                  pl.BlockSpec((d, tf), lambda i, j: (0, j)),
                  pl.BlockSpec((d, tf), lambda i, j: (0, j)),
                  pl.BlockSpec((tf, d), lambda i, j: (j, 0))],
        out_specs=pl.BlockSpec((tm, d), lambda i, j: (i, 0)),
        out_shape=jax.ShapeDtypeStruct((m, d), F32),
        scratch_shapes=[pltpu.VMEM((tm, d), BF16), pltpu.VMEM((tm, d), F32)],
        compiler_params=_cparams(("parallel", "arbitrary")),
        name="ffn",
    )(h, mod, g_pre.reshape(1, d), g_post.reshape(1, d), wg, wu, wd)


def _norm_mm_kernel(*refs, sub, modulated):
    if modulated:
        h_ref, mod_ref, gpre_ref, w_ref, o_ref, xn_ref = refs
    else:
        h_ref, gpre_ref, w_ref, o_ref, xn_ref = refs

    j = pl.program_id(1)

    @pl.when(j == 0)
    def _():
        xn = _rms(h_ref[...]) * gpre_ref[...]
        if modulated:
            xn = xn * (1.0 + mod_ref[0, 3 * sub + 1:3 * sub + 2, :]) + mod_ref[0, 3 * sub:3 * sub + 1, :]
        xn = xn.astype(BF16)
        xn_ref[...] = xn
        o_ref[...] = jnp.dot(xn, w_ref[...], preferred_element_type=F32)

    @pl.when(j > 0)
    def _():
        o_ref[...] = jnp.dot(xn_ref[...], w_ref[...], preferred_element_type=F32)


def _norm_mm(h, mod, mod_row, sub, g_pre, w, tm, tn):
    m, d = h.shape
    n = w.shape[1]
    assert m % tm == 0 and n % tn == 0
    modulated = mod is not None
    mod_specs = [pl.BlockSpec((1, N_MOD, d), lambda i, j: (mod_row(i), 0, 0))] if modulated else []
    return pl.pallas_call(
        functools.partial(_norm_mm_kernel, sub=sub, modulated=modulated),
        grid=(m // tm, n // tn),
        in_specs=[pl.BlockSpec((tm, d), lambda i, j: (i, 0))] + mod_specs
        + [pl.BlockSpec((1, d), lambda i, j: (0, 0)),
           pl.BlockSpec((d, tn), lambda i, j: (0, j))],
        out_specs=pl.BlockSpec((tm, tn), lambda i, j: (i, j)),
        out_shape=jax.ShapeDtypeStruct((m, n), F32),
        scratch_shapes=[pltpu.VMEM((tm, d), BF16)],
        compiler_params=_cparams(("parallel", "arbitrary")),
        name="norm_mm",
    )(*([h] + ([mod] if modulated else []) + [g_pre.reshape(1, d), w]))


def _merge_kernel(br_ref, bm_ref, gr_ref, gm_ref, h_ref, mod_ref, gpost_ref, w_ref, o_ref, *, sub):
    merged = _sigmoid(gr_ref[...]) * br_ref[...] + _sigmoid(gm_ref[...]) * bm_ref[...]
    y = jnp.dot(merged.astype(BF16), w_ref[...], preferred_element_type=F32)
    yn = _rms(y) * gpost_ref[...]
    o_ref[...] = h_ref[...] + mod_ref[0, 3 * sub + 2:3 * sub + 3, :] * yn


def _merge(br, bm, u2d, h, mod, mod_row, sub, g_post, w, tm):
    m, d = h.shape
    tile = pl.BlockSpec((tm, d), lambda i: (i, 0))
    return pl.pallas_call(
        functools.partial(_merge_kernel, sub=sub),
        grid=(m // tm,),
        in_specs=[tile, tile,
                  pl.BlockSpec((tm, d), lambda i: (i, 0)),
                  pl.BlockSpec((tm, d), lambda i: (i, 1)),
                  tile,
                  pl.BlockSpec((1, N_MOD, d), lambda i: (mod_row(i), 0, 0)),
                  pl.BlockSpec((1, d), lambda i: (0, 0)),
                  pl.BlockSpec((d, d), lambda i: (0, 0))],
        out_specs=tile,
        out_shape=jax.ShapeDtypeStruct((m, d), F32),
        compiler_params=_cparams(("parallel",)),
        name="merge",
    )(br, bm, u2d, u2d, h, mod, g_post.reshape(1, d), w)


def _rwkv_feat_kernel(*refs, mode, period, tiles_per_seq, tt):
    if mode == "grid":
        (r_ref, k_ref, v_ref, lo_ref, kprev_ref, vprev_ref, vnext_ref, lonext_ref,
         mur_ref, muk_ref, muv_ref, mulo_ref, wl_ref, pv_ref) = refs[:14]
        rest = refs[14:]
    else:
        (r_ref, k_ref, v_ref, lo_ref, mur_ref, muk_ref, muv_ref, mulo_ref, wl_ref, pv_ref) = refs[:10]
        rest = refs[10:]
    r_o, k_o, v_o, kk_o, b_o, lwf_o, lwb_o, g_o, bonus_o, act_scr = rest
    i = pl.program_id(0)
    j = pl.program_id(1)
    cw = r_ref.shape[1]
    t = lax.broadcasted_iota(jnp.int32, (tt, 1), 0)
    row_first = (t % period) == 0
    row_last = (t % period) == period - 1
    seq_first = (i % tiles_per_seq) == 0
    seq_last = (i % tiles_per_seq) == tiles_per_seq - 1
    quarter = RW_COLS // 4

    def prev1(x):
        return jnp.where(row_first, 0.0, pltpu.roll(x, 1, 0))

    def next1(x):
        return jnp.where(row_last, 0.0, pltpu.roll(x, tt - 1, 0))

    def up(x, halo_ref):
        halo = jnp.where(seq_first, 0.0, halo_ref[...])
        return jnp.concatenate([halo, x[:tt - GRID_W]], axis=0)

    def down(x, halo_ref):
        halo = jnp.where(seq_last, 0.0, halo_ref[...])
        return jnp.concatenate([x[GRID_W:], halo], axis=0)

    def col_index(base, width):
        return base + lax.broadcasted_iota(jnp.int32, (1, width), 1)

    def lerp(x, xs, mu_ref):
        return x + (xs - x) * mu_ref[...]

    @pl.when(j == 0)
    def _():
        x = lo_ref[...]
        xs = down(x, lonext_ref) if mode == "grid" else next1(x)
        x = lerp(x, xs, mulo_ref)
        c = col_index(0, LORA_W)
        act = jnp.where(c < W_LORA, jnp.tanh(x), jnp.where(c < W_LORA + A_LORA, x, _sigmoid(x)))
        act_scr[...] = act.astype(BF16)

    x = r_ref[...]
    c = col_index(j * cw, cw)
    if mode == "grid":
        xs = jnp.where(c < quarter, prev1(x), next1(x))
    else:
        xs = prev1(x)
    r = lerp(x, xs, mur_ref)

    x = k_ref[...]
    c = col_index(RW_DIM + j * cw, cw)
    if mode == "grid":
        xs = jnp.where(c < 2 * quarter, next1(x), up(x, kprev_ref))
    else:
        xs = jnp.where(c < 2 * quarter, prev1(x), next1(x))
    k = lerp(x, xs, muk_ref)

    x = v_ref[...]
    c = col_index(2 * RW_DIM + j * cw, cw)
    if mode == "grid":
        xs = jnp.where(c < 3 * quarter, up(x, vprev_ref), down(x, vnext_ref))
    else:
        xs = next1(x)
    v = lerp(x, xs, muv_ref)

    lo = jnp.dot(act_scr[...], wl_ref[...], preferred_element_type=F32)
    w0f, w0b, a0 = pv_ref[0:1, :], pv_ref[1:2, :], pv_ref[2:3, :]
    k_k, k_a, r_k = pv_ref[3:4, :], pv_ref[4:5, :], pv_ref[5:6, :]
    a = _sigmoid(a0 + lo[:, 2 * cw:3 * cw])
    ones = _group_ones(cw, RW_HEAD)
    kkr = k * k_k
    kk = kkr * lax.rsqrt(jnp.maximum(_group_sum(kkr * kkr, ones), L2_EPS * L2_EPS))
    k = k * (1.0 + (a - 1.0) * k_a)
    lwf_o[...] = -math.exp(-0.5) * _sigmoid(w0f + lo[:, 0:cw])
    lwb_o[...] = -math.exp(-0.5) * _sigmoid(w0b + lo[:, cw:2 * cw])
    r_o[...] = r.astype(r_o.dtype)
    k_o[...] = k.astype(k_o.dtype)
    v_o[...] = v.astype(v_o.dtype)
    kk_o[...] = kk.astype(kk_o.dtype)
    b_o[...] = (kk * a).astype(b_o.dtype)
    g_o[...] = lo[:, 3 * cw:4 * cw]
    bonus_o[...] = _group_sum(r * k * r_k, ones) * v


def _rwkv_feat(u2d, mode, seq_len, tt, mu, w_lora, pv):
    m = u2d.shape[0]
    cw = FEAT_CW
    ncb = RW_DIM // cw
    assert m % tt == 0 and seq_len % tt == 0
    tiles_per_seq = seq_len // tt
    rb, kb, vb = RW_BASE // cw, (RW_BASE + RW_DIM) // cw, (RW_BASE + 2 * RW_DIM) // cw
    lob = LORA_BASE // LORA_W
    hr = tt // GRID_W
    nhb = m // GRID_W
    main = [pl.BlockSpec((tt, cw), lambda i, j: (i, rb + j)),
            pl.BlockSpec((tt, cw), lambda i, j: (i, kb + j)),
            pl.BlockSpec((tt, cw), lambda i, j: (i, vb + j)),
            pl.BlockSpec((tt, LORA_W), lambda i, j: (i, lob))]
    args = [u2d, u2d, u2d, u2d]
    if mode == "grid":
        assert tt % GRID_W == 0
        period = GRID_W
        prev_row = lambda i: jnp.maximum(i * hr - 1, 0)
        next_row = lambda i: jnp.minimum((i + 1) * hr, nhb - 1)
        main += [pl.BlockSpec((GRID_W, cw), lambda i, j: (prev_row(i), kb + j)),
                 pl.BlockSpec((GRID_W, cw), lambda i, j: (prev_row(i), vb + j)),
                 pl.BlockSpec((GRID_W, cw), lambda i, j: (next_row(i), vb + j)),
                 pl.BlockSpec((GRID_W, LORA_W), lambda i, j: (next_row(i), lob))]
        args += [u2d, u2d, u2d, u2d]
    else:
        assert tiles_per_seq == 1
        period = tt
    params = [pl.BlockSpec((1, cw), lambda i, j: (0, j)),
              pl.BlockSpec((1, cw), lambda i, j: (0, RW_DIM // cw + j)),
              pl.BlockSpec((1, cw), lambda i, j: (0, 2 * RW_DIM // cw + j)),
              pl.BlockSpec((1, LORA_W), lambda i, j: (0, 3 * RW_DIM // LORA_W)),
              pl.BlockSpec((None, LORA_W, 4 * cw), lambda i, j: (j, 0, 0)),
              pl.BlockSpec((SUBLANES, cw), lambda i, j: (0, j))]
    args += [mu, mu, mu, mu, w_lora, pv]
    out_spec = pl.BlockSpec((tt, cw), lambda i, j: (i, j))
    return pl.pallas_call(
        functools.partial(_rwkv_feat_kernel, mode=mode, period=period,
                          tiles_per_seq=tiles_per_seq, tt=tt),
        grid=(m // tt, ncb),
        in_specs=main + params,
        out_specs=[out_spec] * 9,
        out_shape=[jax.ShapeDtypeStruct((m, RW_DIM), BF16)] * 5
        + [jax.ShapeDtypeStruct((m, RW_DIM), F32)] * 4,
        scratch_shapes=[pltpu.VMEM((tt, LORA_W), BF16)],
        compiler_params=_cparams(("parallel", "arbitrary")),
        name="rwkv_feat",
    )(*args)


def _conv_kernel(x_ref, prev_ref, next_ref, w_ref, o_ref, *, tiles_per_seq, tt):
    i = pl.program_id(0)
    x = x_ref[...]
    t = lax.broadcasted_iota(jnp.int32, (tt, 1), 0)
    seq_first = (i % tiles_per_seq) == 0
    seq_last = (i % tiles_per_seq) == tiles_per_seq - 1
    hp = jnp.where(seq_first, 0.0, prev_ref[SUBLANES - 1:SUBLANES, :])
    hn = jnp.where(seq_last, 0.0, next_ref[0:1, :])
    prev = jnp.where(t == 0, hp, pltpu.roll(x, 1, 0))
    nxt = jnp.where(t == tt - 1, hn, pltpu.roll(x, tt - 1, 0))
    y = prev * w_ref[0:1, :] + x * w_ref[1:2, :] + nxt * w_ref[2:3, :] + w_ref[3:4, :]
    o_ref[...] = y * _sigmoid(y)


def _conv(u2d, seq_len, tt, wconv):
    m = u2d.shape[0]
    cw = CONV_CW
    base = XBC_BASE // cw
    tiles_per_seq = seq_len // tt
    hr = tt // SUBLANES
    nhb = m // SUBLANES
    return pl.pallas_call(
        functools.partial(_conv_kernel, tiles_per_seq=tiles_per_seq, tt=tt),
        grid=(m // tt, M_XBC // cw),
        in_specs=[pl.BlockSpec((tt, cw), lambda i, j: (i, base + j)),
                  pl.BlockSpec((SUBLANES, cw), lambda i, j: (jnp.maximum(i * hr - 1, 0), base + j)),
                  pl.BlockSpec((SUBLANES, cw),
                               lambda i, j: (jnp.minimum((i + 1) * hr, nhb - 1), base + j)),
                  pl.BlockSpec((SUBLANES, cw), lambda i, j: (0, j))],
        out_specs=pl.BlockSpec((tt, cw), lambda i, j: (i, j)),
        out_shape=jax.ShapeDtypeStruct((m, M_XBC), F32),
        compiler_params=_cparams(("parallel", "parallel")),
        name="conv",
    )(u2d, u2d, u2d, wconv)


def _stack2(x, m0):
    return jnp.concatenate([jnp.where(m0, x, 0.0), jnp.where(m0, 0.0, x)], axis=0)


def _rwkv_kernel(*refs, nchunks, pp, reverse, emit, finish):
    r_ref, k_ref, v_ref, kk_ref, b_ref, lw_ref, s0_ref = refs[:7]
    rest = list(refs[7:])
    if finish:
        yacc_ref, bonus_ref, g_ref, lng_ref, lnb_ref = rest[:5]
        rest = rest[5:]
    y_ref = rest.pop(0) if emit else None
    sfin_ref, s_scr = rest
    i = pl.program_id(2)

    @pl.when(i == 0)
    def _():
        s_scr[...] = s0_ref[...]

    lane = lax.broadcasted_iota(jnp.int32, (CHUNK, LANES), 1)
    trow = lax.broadcasted_iota(jnp.int32, (CHUNK, LANES), 0)
    m0 = lane < RW_HEAD
    scol = lane % RW_HEAD
    strict = (scol > trow) if reverse else (scol < trow)
    incl = (scol >= trow) if reverse else (scol <= trow)
    eye = (scol == trow).astype(F32)
    row2 = lax.broadcasted_iota(jnp.int32, (LANES, LANES), 0)
    col2 = lax.broadcasted_iota(jnp.int32, (LANES, LANES), 1)
    same = (row2 // RW_HEAD) == (col2 // RW_HEAD)
    tri3 = _tri3(reverse)
    last = 0 if reverse else CHUNK - 1
    bf = lambda x: x.astype(BF16)
    st2 = lambda x: _stack2(x, m0)

    sls = [(slice(c * CHUNK, (c + 1) * CHUNK), slice(hp * LANES, (hp + 1) * LANES))
           for c in range(nchunks) for hp in range(pp)]
    G = range(nchunks * pp)
    lw = [lw_ref[sl] for sl in sls]
    cum = [_dot_01_lhs(tri3, x) for x in lw]
    tot = [c[last:last + 1, :] for c in cum]
    eneg = [jnp.exp(-c) for c in cum]
    at = [bf(-kk_ref[sl] * jnp.exp(c - x)) for sl, c, x in zip(sls, cum, lw)]
    bk2 = [jnp.concatenate([st2(bf(b_ref[sl] * e)), st2(bf(k_ref[sl] * e))], axis=0)
           for sl, e in zip(sls, eneg)]
    if emit:
        rt = [r_ref[sl] * jnp.exp(c) for sl, c in zip(sls, cum)]
        big = [_dot_nt(jnp.concatenate([x, bf(y)], axis=0), z) for x, y, z in zip(at, rt, bk2)]
        rb = [bf(jnp.where(incl, x[CHUNK:, :LANES], 0.0)) for x in big]
        rk = [bf(jnp.where(incl, x[CHUNK:, LANES:], 0.0)) for x in big]
    else:
        big = [_dot_nt(x, z) for x, z in zip(at, bk2)]
    n = [jnp.where(strict, x[:CHUNK, :LANES], 0.0) for x in big]
    ak = [bf(jnp.where(strict, x[:CHUNK, LANES:], 0.0)) for x in big]
    v = [bf(v_ref[sl]) for sl in sls]
    v2 = [st2(x) for x in v]
    akv = [bf(_dot(x, y)) for x, y in zip(ak, v2)]
    p = [eye + x for x in n]
    npow = [bf(x) for x in n]
    npow = [bf(_dot(x, st2(x))) for x in npow]
    for it in range(4):
        both = [_dot(jnp.concatenate([x, bf(y)], axis=0), st2(x)) for x, y in zip(npow, p)]
        p = [y + x[CHUNK:] for x, y in zip(both, p)]
        npow = [bf(x[:CHUNK]) for x in both]
    p = [bf(y + _dot(bf(y), st2(x))) for x, y in zip(npow, p)]
    wu = [_dot(x, jnp.concatenate([st2(y), st2(z)], axis=1)) for x, y, z in zip(p, at, akv)]
    w = [bf(x[:, :LANES]) for x in wu]
    up = [bf(x[:, LANES:]) for x in wu]
    eend = [jnp.exp(t - c) for t, c in zip(tot, cum)]
    bg = [bf(b_ref[sl] * e) for sl, e in zip(sls, eend)]
    kg = [bf(k_ref[sl] * e) for sl, e in zip(sls, eend)]
    mc = [bf(jnp.where(same, _dot_tn(x, y), 0.0)) for x, y in zip(w, bg)]
    nc = [jnp.where(same, _dot_tn(jnp.concatenate([x, y], axis=0),
                                  jnp.concatenate([z, t], axis=0)), 0.0)
          for x, y, z, t in zip(up, v, bg, kg)]
    gam = [jnp.exp(t) for t in tot]
    if emit:
        q = [bf(x + _dot(y, st2(z))) for x, y, z in zip(rt, rb, w)]
        yp = [_dot(jnp.concatenate([x, y], axis=1), jnp.concatenate([st2(z), t], axis=0))
              for x, y, z, t in zip(rb, rk, up, v2)]

    s = [s_scr[hp] for hp in range(pp)]
    ys = [None] * len(G)
    for c in (reversed(range(nchunks)) if reverse else range(nchunks)):
        for hp in range(pp):
            g = c * pp + hp
            sb = s[hp].astype(BF16)
            if emit:
                ys[g] = _dot_nt(q[g], sb) + yp[g]
            s[hp] = s[hp] * gam[g] + _dot(sb, mc[g]) + nc[g]
    for hp in range(pp):
        s_scr[hp] = s[hp]

    if emit and not finish:
        for g in G:
            y_ref[sls[g]] = ys[g]
    if finish:
        ones = _group_ones(LANES, RW_HEAD)
        ysum = [y + yacc_ref[sl] for y, sl in zip(ys, sls)]
        mean = [_group_sum(y, ones) * (1.0 / RW_HEAD) for y in ysum]
        dev = [y - mu for y, mu in zip(ysum, mean)]
        var = [_group_sum(d * d, ones) * (1.0 / RW_HEAD) for d in dev]
        for g in G:
            yn = dev[g] * lax.rsqrt(var[g] + GN_EPS)
            lanes = sls[g][1]
            o = (yn * lng_ref[:, lanes] + lnb_ref[:, lanes] + bonus_ref[sls[g]]) * g_ref[sls[g]]
            y_ref[sls[g]] = o.astype(y_ref.dtype)

    @pl.when(i == pl.num_programs(2) - 1)
    def _():
        for hp in range(pp):
            sfin_ref[hp] = s[hp]


RW_PAIRS_PER_STEP = 4


def _rwkv_scan(r, k, v, kk, b, lw, s0, tb, reverse, emit=True, finish=None):
    bsz, l, _ = r.shape
    pp = RW_PAIRS_PER_STEP
    npair = RW_DIM // LANES
    nblk = l // tb
    assert l % tb == 0 and tb % CHUNK == 0 and npair % pp == 0
    blk = (lambda i: nblk - 1 - i) if reverse else (lambda i: i)
    seq = pl.BlockSpec((None, tb, pp * LANES), lambda bi, p, i: (bi, blk(i), p))
    st = pl.BlockSpec((None, pp, LANES, LANES), lambda bi, p, i: (bi, p, 0, 0))
    vec = pl.BlockSpec((1, pp * LANES), lambda bi, p, i: (0, p))
    st_shape = jax.ShapeDtypeStruct((bsz, npair, LANES, LANES), F32)
    y_shape = jax.ShapeDtypeStruct((bsz, l, RW_DIM), BF16 if finish else F32)
    args = [r, k, v, kk, b, lw, s0]
    in_specs = [seq] * 6 + [st]
    if finish:
        y_other, bonus, g, ln_g, ln_b = finish
        args += [y_other, bonus, g, ln_g.reshape(1, RW_DIM), ln_b.reshape(1, RW_DIM)]
        in_specs += [seq, seq, seq, vec, vec]
    outs = pl.pallas_call(
        functools.partial(_rwkv_kernel, nchunks=tb // CHUNK, pp=pp, reverse=reverse, emit=emit,
                          finish=bool(finish)),
        grid=(bsz, npair // pp, nblk),
        in_specs=in_specs,
        out_specs=([seq] if emit else []) + [st],
        out_shape=([y_shape] if emit else []) + [st_shape],
        scratch_shapes=[pltpu.VMEM((pp, LANES, LANES), F32)],
        compiler_params=_cparams(("parallel", "parallel", "arbitrary")),
        name="rwkv_scan",
    )(*args)
    return outs if emit else (None, outs[0])


def _ssd_kernel(*refs, nchunks, reverse, emit, finish):
    x_ref, b_ref, c_ref, udt_ref, bias_ref, aall_ref, s0_ref = refs[:7]
    rest = list(refs[7:])
    if finish:
        yacc_ref, z_ref, dskip_ref = rest[:3]
        rest = rest[3:]
    y_ref = rest.pop(0) if emit else None
    sfin_ref, s_scr = rest
    grp = pl.program_id(1)
    i = pl.program_id(2)

    @pl.when(i == 0)
    def _():
        s_scr[...] = s0_ref[...]

    width = M_HPG * M_HEADDIM
    m0 = lax.broadcasted_iota(jnp.int32, (CHUNK, LANES), 1) < M_HEADDIM
    tri3 = _tri3(reverse)
    qr = lax.broadcasted_iota(jnp.int32, (3 * CHUNK, CHUNK), 0) % CHUNK
    qc = lax.broadcasted_iota(jnp.int32, (3 * CHUNK, CHUNK), 1)
    tri3_t = ((qr >= qc) if reverse else (qr <= qc)).astype(BF16)
    head0 = (M_HEADS if reverse else 0) + grp * M_HPG
    hr = lax.broadcasted_iota(jnp.int32, (3 * LANES, M_HPG), 0) % LANES
    hc = lax.broadcasted_iota(jnp.int32, (3 * LANES, M_HPG), 1)
    sel3 = (hr == head0 + hc).astype(BF16)
    er = lax.broadcasted_iota(jnp.int32, (3 * M_HPG, width), 0) % M_HPG
    ec = lax.broadcasted_iota(jnp.int32, (3 * M_HPG, width), 1) // M_HEADDIM
    expand3 = (er == ec).astype(BF16)
    lt = lax.broadcasted_iota(jnp.int32, (CHUNK, LANES), 0)
    ll = lax.broadcasted_iota(jnp.int32, (CHUNK, LANES), 1)
    keep = ((ll % CHUNK) >= lt) if reverse else ((ll % CHUNK) <= lt)
    last = 0 if reverse else CHUNK - 1
    split_cat = lambda x, axis: jnp.concatenate(_split3(x), axis=axis)

    G = range(nchunks)
    sls = [slice(g * CHUNK, (g + 1) * CHUNK) for g in G]
    dt_all = [_softplus(udt_ref[sl] + bias_ref[...]) for sl in sls]
    both = [jnp.dot(split_cat(jnp.concatenate([x, x * aall_ref[...]], axis=0), 1), sel3,
                    preferred_element_type=F32) for x in dt_all]
    dt = [x[:CHUNK] for x in both]
    da = [x[CHUNK:] for x in both]
    cum = [_dot_01_lhs(tri3, x) for x in da]
    dt_x = [jnp.dot(split_cat(x, 1), expand3, preferred_element_type=F32) for x in dt]
    cum_x = [jnp.dot(split_cat(x, 1), expand3, preferred_element_type=F32) for x in cum]
    tot_x = [x[last:last + 1, :] for x in cum_x]
    xdt = [x_ref[sl] * y for sl, y in zip(sls, dt_x)]
    xend = [x * jnp.exp(t - c) for x, t, c in zip(xdt, tot_x, cum_x)]
    bm = [b_ref[sl].astype(BF16) for sl in sls]
    cs = [_dot_tn(x, y) for x, y in zip(bm, xend)]
    gam = [jnp.exp(t) for t in tot_x]
    if emit:
        cm = [c_ref[sl].astype(BF16) for sl in sls]
        cum_t = [lax.dot_general(split_cat(x, 0), tri3_t, (((0,), (0,)), ((), ())),
                                 preferred_element_type=F32) for x in da]
        cb2 = [_dot_nt(x, jnp.concatenate([y, y], axis=0)) for x, y in zip(cm, bm)]
        ydiag = []
        for g in G:
            ys = []
            for j in range(M_HPG // 2):
                colsel = cum_x[g][:, j * LANES:(j + 1) * LANES]
                rowsel = jnp.concatenate([cum_t[g][2 * j:2 * j + 1, :],
                                          cum_t[g][2 * j + 1:2 * j + 2, :]], axis=1)
                seg = jnp.where(keep, colsel - rowsel, -jnp.inf)
                gp = cb2[g] * jnp.exp(seg)
                xp = xdt[g][:, j * LANES:(j + 1) * LANES]
                ys.append(_dot(gp, _stack2(xp, m0)))
            ydiag.append(jnp.concatenate(ys, axis=1))
        ecum = [jnp.exp(x) for x in cum_x]

    s = s_scr[...]
    for g in (reversed(G) if reverse else G):
        if emit:
            y = _dot(cm[g], s) * ecum[g] + ydiag[g]
            if finish:
                zz = z_ref[sls[g]]
                y = (y + yacc_ref[sls[g]] + dskip_ref[...] * x_ref[sls[g]]) * (zz * _sigmoid(zz))
            y_ref[sls[g]] = y
        s = s * gam[g] + cs[g]
    s_scr[...] = s

    @pl.when(i == pl.num_programs(2) - 1)
    def _():
        sfin_ref[...] = s


def _ssd_scan(xbc, u, dt_bias, a_all, s0, tb, reverse, emit=True, finish=None):
    bsz, l, _ = xbc.shape
    ncb = tb // CHUNK
    nblk = l // tb
    width = M_HPG * M_HEADDIM
    blk = (lambda i: nblk - 1 - i) if reverse else (lambda i: i)
    st = pl.BlockSpec((None, None, M_STATE, width), lambda bi, g, i: (bi, g, 0, 0))
    xspec = pl.BlockSpec((None, tb, width), lambda bi, g, i: (bi, blk(i), g))
    bspec = pl.BlockSpec((None, tb, M_STATE), lambda bi, g, i: (bi, blk(i), M_DIM // M_STATE + g))
    cspec = pl.BlockSpec((None, tb, M_STATE),
                         lambda bi, g, i: (bi, blk(i), (M_DIM + M_BC) // M_STATE + g))
    vec = pl.BlockSpec((1, 2 * M_HEADS), lambda bi, g, i: (0, 0))
    args = [xbc, xbc, xbc, u, dt_bias, a_all, s0]
    in_specs = [xspec, bspec, cspec,
                pl.BlockSpec((None, tb, 2 * M_HEADS), lambda bi, g, i: (bi, blk(i), DT_BASE // (2 * M_HEADS))),
                vec, vec, st]
    aliases = {}
    if finish:
        y_other, dskip = finish
        args += [y_other, u, dskip]
        in_specs += [xspec,
                     pl.BlockSpec((None, tb, width), lambda bi, g, i: (bi, blk(i), Z_BASE // width + g)),
                     pl.BlockSpec((1, width), lambda bi, g, i: (0, g))]
        aliases = {7: 0}
    outs = pl.pallas_call(
        functools.partial(_ssd_kernel, nchunks=ncb, reverse=reverse, emit=emit,
                          finish=bool(finish)),
        grid=(bsz, M_GROUPS, nblk),
        in_specs=in_specs,
        out_specs=([xspec] if emit else []) + [st],
        out_shape=([jax.ShapeDtypeStruct((bsz, l, M_DIM), F32)] if emit else [])
        + [jax.ShapeDtypeStruct((bsz, M_GROUPS, M_STATE, width), F32)],
        scratch_shapes=[pltpu.VMEM((M_STATE, width), F32)],
        input_output_aliases=aliases,
        compiler_params=_cparams(("parallel", "parallel", "arbitrary")),
        name="ssd_scan",
    )(*args)
    return outs if emit else (None, outs[0])


def kernel(x, c, ctx, c_ctx, ada_w, ada_b, norm_pre, norm_post, ffn1_gate, ffn1_up, ffn1_down,
           ffn2_gate, ffn2_up, ffn2_down, w_in, rw_mu, rw_w0, rw_w_up, rw_a0, rw_a_up, rw_g_up,
           rw_k_k, rw_k_a, rw_r_k, rw_ln_g, rw_ln_b, m_conv_w, m_conv_b, m_dt_bias, m_A_log, m_D,
           m_norm, w_br, w_bm, w_out):
    bsz, l, d = x.shape
    lc = ctx.shape[1]
    i = 0
    tm = 512
    tpb = l // tm

    cc = jnp.concatenate([c, c_ctx[None], jnp.zeros((SUBLANES - bsz - 1, d), F32)], axis=0)
    mod = _mm(jax.nn.silu(cc), ada_w[i], SUBLANES, 1024) + ada_b[i]
    mod = mod.reshape(SUBLANES, N_MOD, d)
    lat_row = lambda t: t // tpb
    ctx_row = lambda t: bsz

    bf = lambda w: w.astype(BF16)
    h = x.reshape(bsz * l, d)
    hc = ctx.reshape(bsz * lc, d)
    wg1, wu1, wd1 = bf(ffn1_gate[i]), bf(ffn1_up[i]), bf(ffn1_down[i])
    h = _ffn(h, mod, lat_row, 0, norm_pre[i, 0], norm_post[i, 0], wg1, wu1, wd1, tm, 512)
    hc = _ffn(hc, mod, ctx_row, 0, norm_pre[i, 0], norm_post[i, 0], wg1, wu1, wd1, tm, 512)

    w_cat = _regroup(w_in, i)
    tn = 1920
    u2d = _norm_mm(h, mod, lat_row, 1, norm_pre[i, 1], w_cat, tm, tn)
    uc2d = _norm_mm(hc, mod, ctx_row, 1, norm_pre[i, 1], w_cat, bsz * lc, tn // 3)
    u = u2d.reshape(bsz, l, U_COLS)
    uc = uc2d.reshape(bsz, lc, U_COLS)

    cw = FEAT_CW
    mu = jnp.pad(rw_mu[i], (0, 3 * RW_DIM + LORA_W - RW_COLS)).reshape(1, -1)
    zero = lambda rows: jnp.zeros((rows, RW_DIM), F32)
    wl_rows = lambda w, before, after: jnp.concatenate([zero(before), w, zero(after)], axis=0)
    w_lora = jnp.stack([wl_rows(rw_w_up[i, 0], 0, LORA_W - W_LORA),
                        wl_rows(rw_w_up[i, 1], 0, LORA_W - W_LORA),
                        wl_rows(rw_a_up[i], W_LORA, LORA_W - W_LORA - A_LORA),
                        wl_rows(rw_g_up[i], W_LORA + A_LORA, LORA_W - W_LORA - A_LORA - G_LORA)],
                       axis=1)
    w_lora = (w_lora.reshape(LORA_W, 4, RW_DIM // cw, cw).transpose(2, 0, 1, 3)
              .reshape(RW_DIM // cw, LORA_W, 4 * cw).astype(BF16))
    pv = jnp.stack([rw_w0[i, 0], rw_w0[i, 1], rw_a0[i], rw_k_k[i], rw_k_a[i],
                    rw_r_k[i].reshape(RW_DIM), jnp.zeros((RW_DIM,), F32),
                    jnp.zeros((RW_DIM,), F32)], axis=0)
    seq3 = lambda t, n: t.reshape(bsz, n, RW_DIM)
    r, k, v, kk, bvec, lwf, lwb, g, bonus = [
        seq3(t, l) for t in _rwkv_feat(u2d, "grid", l, tm, mu, w_lora, pv)]
    rc, kc, vc, kkc, bc, lwfc, lwbc, _, _ = [
        seq3(t, lc) for t in _rwkv_feat(uc2d, "seq", lc, lc, mu, w_lora, pv)]
    s0r = jnp.zeros((bsz, RW_DIM // LANES, LANES, LANES), F32)
    _, src_f = _rwkv_scan(rc, kc, vc, kkc, bc, lwfc, s0r, lc, False, emit=False)
    _, src_b = _rwkv_scan(rc, kc, vc, kkc, bc, lwbc, s0r, lc, True, emit=False)
    yr, _ = _rwkv_scan(r, k, v, kk, bvec, lwf, src_f, tm, False)
    o_r, _ = _rwkv_scan(r, k, v, kk, bvec, lwb, src_b, tm, True,
                        finish=(yr, bonus, g, rw_ln_g[i], rw_ln_b[i]))
    br = _mm(o_r.reshape(bsz * l, RW_DIM), bf(w_br[i]), tm, D_MODEL)

    wconv = jnp.concatenate([m_conv_w[i], m_conv_b[i][None],
                             jnp.zeros((SUBLANES - 4, M_XBC), F32)], axis=0)
    xbc = _conv(u2d, l, min(l, 4 * tm), wconv).reshape(bsz, l, M_XBC)
    xbcc = _conv(uc2d, lc, lc, wconv).reshape(bsz, lc, M_XBC)
    dt_bias = m_dt_bias[i].reshape(1, 2 * M_HEADS)
    a_all = -jnp.exp(m_A_log[i]).reshape(1, 2 * M_HEADS)
    s0m = jnp.zeros((bsz, M_GROUPS, M_STATE, M_HPG * M_HEADDIM), F32)
    _, smc_f = _ssd_scan(xbcc, uc, dt_bias, a_all, s0m, lc, False, emit=False)
    _, smc_b = _ssd_scan(xbcc, uc, dt_bias, a_all, s0m, lc, True, emit=False)
    ym, _ = _ssd_scan(xbc, u, dt_bias, a_all, smc_f, tm, False)
    dskip = jnp.repeat(m_D[i], M_HEADDIM).reshape(1, M_DIM)
    ym, _ = _ssd_scan(xbc, u, dt_bias, a_all, smc_b, tm, True, finish=(ym, dskip))
    bmo = _norm_mm(ym.reshape(bsz * l, M_DIM), None, None, 0, m_norm[i], bf(w_bm[i]), tm, 512)

    h = _merge(br, bmo, u2d, h, mod, lambda t: t // (l // 256), 1, norm_post[i, 1], bf(w_out[i]), 256)
    h = _ffn(h, mod, lat_row, 2, norm_pre[i, 2], norm_post[i, 2], bf(ffn2_gate[i]),
             bf(ffn2_up[i]), bf(ffn2_down[i]), tm, 512)
    return h.reshape(bsz, l, d)
```
